```python
import functools
import jax, jax.numpy as jnp
from jax import lax
import numpy as np

D_MODEL = 1024
BATCH = 2
SEQ = 8192
DEPTH = 4
DEC_BATCH = 4
DEC_SEQ = 8192
PAST_LEN = 128

BRANCH_WIDTH = 512
N_BRANCH = 3
A_HEADS = 4
A_HEAD_K = 128
A_HEAD_V = BRANCH_WIDTH // A_HEADS
A_KEY = A_HEADS * A_HEAD_K
A_VAL = BRANCH_WIDTH
B_HEADS = 4
B_HEAD_V = BRANCH_WIDTH // B_HEADS
B_HEAD_K = B_HEAD_V // 2
B_KEY = B_HEADS * B_HEAD_K
B_VAL = BRANCH_WIDTH
GLA_RANK = 16
GLA_NORMALIZER = 16.0
C_WIDTH = BRANCH_WIDTH
C_BLOCKS = 8
C_BLOCK = C_WIDTH // C_BLOCKS
C_CONV = 4
RG_C = 8.0
D_FF = 2816
FFN_CONV = 3
CHUNK = 16
EPS = 1e-6

IN_SPLITS = (A_KEY, A_KEY, A_KEY, A_VAL, A_VAL,
             B_KEY, B_KEY, B_VAL, B_VAL, GLA_RANK, GLA_RANK,
             C_WIDTH, C_WIDTH,
             D_MODEL, D_MODEL, D_MODEL)
N_IN = sum(IN_SPLITS)

kernel_name = 'hybrid_hgrn2_gla_rglru_encoder'


def rms_norm(x, w):
    xf = x.astype(jnp.float32)
    y = xf * lax.rsqrt(jnp.mean(xf * xf, axis=-1, keepdims=True) + EPS)
    return (y * w.astype(jnp.float32)).astype(x.dtype)


def dw_conv(x, w, b, left):
    K = w.shape[0]
    T = x.shape[1]
    xp = jnp.pad(x, ((0, 0), (left, K - 1 - left), (0, 0)))
    out = b
    for j in range(K):
        out = out + xp[:, j:j + T] * w[j]
    return out


def gla_chunk(q, k, v, log_g):
    B, T, H, dk = q.shape
    n = T // CHUNK

    def blk(t):
        return t.astype(jnp.float32).reshape(B, n, CHUNK, H, t.shape[-1])

    q, k, v, log_g = blk(q) * (dk ** -0.5), blk(k), blk(v), blk(log_g)
    b = jnp.cumsum(log_g, axis=2)
    b_last = b[:, :, -1]
    q_dec = q * jnp.exp(b)
    att = jnp.einsum('bnihd,bnjhd->bnhij', q_dec, k * jnp.exp(-b))
    att = jnp.where(jnp.tril(jnp.ones((CHUNK, CHUNK), dtype=bool)), att, 0.0)
    o_intra = jnp.einsum('bnhij,bnjhe->bnihe', att, v)
    k_tail = k * jnp.exp(b_last[:, :, None] - b)

    def step(S, xs):
        qd, kt, vc, bl = xs
        o = jnp.einsum('bihd,bhde->bihe', qd, S)
        S = S * jnp.exp(bl)[..., None] + jnp.einsum('bjhd,bjhe->bhde', kt, vc)
        return S, o

    S0 = jnp.zeros((B, H, dk, v.shape[-1]), jnp.float32)
    xs = tuple(jnp.moveaxis(t, 1, 0) for t in (q_dec, k_tail, v, b_last))
    _, o_inter = lax.scan(step, S0, xs)
    return (o_intra + jnp.moveaxis(o_inter, 0, 1)).reshape(B, T, H, -1)


def bidir_gla(q, k_f, k_b, v, lg_f, lg_b):
    flip = lambda t: jnp.flip(t, axis=1)
    return gla_chunk(q, k_f, v, lg_f) + flip(gla_chunk(flip(q), flip(k_b), flip(v), flip(lg_b)))


def _lin_combine(left, right):
    a1, b1 = left
    a2, b2 = right
    return a1 * a2, a2 * b1 + b2


def rglru(x, wa, ba, wx, bx, lam, reverse):
    B, T, C = x.shape
    xb = x.reshape(B, T, C_BLOCKS, C_BLOCK)
    r = jax.nn.sigmoid(jnp.einsum('btnk,nkj->btnj', xb, wa).reshape(B, T, C) + ba)
    i = jax.nn.sigmoid(jnp.einsum('btnk,nkj->btnj', xb, wx).reshape(B, T, C) + bx)
    log_a = -RG_C * r * jax.nn.softplus(-lam)
    a = jnp.exp(log_a)
    u = jnp.sqrt(-jnp.expm1(2.0 * log_a)) * (i * x)
    _, h = lax.associative_scan(_lin_combine, (a, u), axis=1, reverse=reverse)
    return h


def token_mixers(h, w_in, lb, hgrn_norm_w, gla_up_w, gla_up_b, gla_norm_w,
                 c_conv_w, c_conv_b, rglru_wa, rglru_ba, rglru_wx, rglru_bx, rglru_lam,
                 w_branch, w_out):
    B, T, _ = h.shape
    f32 = lambda t: t.astype(jnp.float32)
    heads = lambda t, n: t.reshape(B, T, n, -1)
    parts = jnp.split(h @ w_in, np.cumsum(IN_SPLITS)[:-1].tolist(), axis=-1)
    (q_a, zf_a, zb_a, i_a, og_a, q_b, k_b, v_b, og_b, lrf_b, lrb_b,
     x_c, y_c, g_a, g_b, g_c) = parts

    lbf = f32(lb)
    lg_af = jnp.log(lbf[0] + (1.0 - lbf[0]) * jax.nn.sigmoid(f32(zf_a)))
    lg_ab = jnp.log(lbf[1] + (1.0 - lbf[1]) * jax.nn.sigmoid(f32(zb_a)))
    o_a = bidir_gla(heads(jax.nn.silu(f32(q_a)), A_HEADS),
                    heads(-jnp.expm1(lg_af), A_HEADS), heads(-jnp.expm1(lg_ab), A_HEADS),
                    heads(f32(i_a), A_HEADS),
                    heads(lg_af, A_HEADS), heads(lg_ab, A_HEADS))
    o_a = (rms_norm(o_a, hgrn_norm_w).reshape(B, T, A_VAL) * jax.nn.silu(f32(og_a))).astype(h.dtype)

    up_w, up_b = f32(gla_up_w), f32(gla_up_b)
    lg_bf = jax.nn.log_sigmoid(f32(lrf_b) @ up_w[0] + up_b[0]) / GLA_NORMALIZER
    lg_bb = jax.nn.log_sigmoid(f32(lrb_b) @ up_w[1] + up_b[1]) / GLA_NORMALIZER
    kb = heads(f32(k_b), B_HEADS)
    o_b = bidir_gla(heads(f32(q_b), B_HEADS), kb, kb, heads(f32(v_b), B_HEADS),
                    heads(lg_bf, B_HEADS), heads(lg_bb, B_HEADS))
    o_b = (rms_norm(o_b, gla_norm_w).reshape(B, T, B_VAL) * jax.nn.silu(f32(og_b))).astype(h.dtype)

    xc = f32(dw_conv(x_c, c_conv_w, c_conv_b, C_CONV // 2))
    hc = (rglru(xc, rglru_wa[0], rglru_ba[0], rglru_wx[0], rglru_bx[0], rglru_lam[0], False)
          + rglru(xc, rglru_wa[1], rglru_ba[1], rglru_wx[1], rglru_bx[1], rglru_lam[1], True))
    o_c = (hc * jax.nn.gelu(f32(y_c))).astype(h.dtype)

    m = (jax.nn.sigmoid(g_a) * (o_a @ w_branch[0])
         + jax.nn.sigmoid(g_b) * (o_b @ w_branch[1])
         + jax.nn.sigmoid(g_c) * (o_c @ w_branch[2]))
    return m @ w_out


def conv_ffn(h, ffn_up, ffn_conv_w, ffn_conv_b, ffn_down):
    u = dw_conv(h @ ffn_up, ffn_conv_w, ffn_conv_b, FFN_CONV // 2)
    gate, val = jnp.split(u, 2, axis=-1)
    return (jax.nn.gelu(gate) * val) @ ffn_down


def trunk(x, norm_mix_w, w_in, hgrn_lb_logits, hgrn_norm_w, gla_up_w, gla_up_b, gla_norm_w,
          c_conv_w, c_conv_b, rglru_wa, rglru_ba, rglru_wx, rglru_bx, rglru_lam,
          w_branch, w_out, norm_ffn_w, ffn_up, ffn_conv_w, ffn_conv_b, ffn_down, final_norm_w):
    lb_all = jnp.cumsum(jax.nn.softmax(hgrn_lb_logits.astype(jnp.float32), axis=0), axis=0)
    lb_all = lb_all - lb_all[0]
    for l in range(DEPTH):
        h = rms_norm(x, norm_mix_w[l])
        x = x + token_mixers(h, w_in[l], lb_all[l], hgrn_norm_w[l], gla_up_w[l], gla_up_b[l],
                             gla_norm_w[l], c_conv_w[l], c_conv_b[l], rglru_wa[l], rglru_ba[l],
                             rglru_wx[l], rglru_bx[l], rglru_lam[l], w_branch[l], w_out[l]).astype(x.dtype)
        h = rms_norm(x, norm_ffn_w[l])
        x = x + conv_ffn(h, ffn_up[l], ffn_conv_w[l], ffn_conv_b[l], ffn_down[l]).astype(x.dtype)
    return rms_norm(x, final_norm_w)


def setup_inputs(seed: int = 0) -> dict:
    key = jax.random.key(seed)
    ks = jax.random.split(key, 26)
    nrm = lambda k, shape, scale: jax.random.normal(k, shape, jnp.float32) * scale
    gain = lambda k, shape: 1.0 + 0.05 * jax.random.normal(k, shape, jnp.float32)
    u = jax.random.uniform(ks[15], (DEPTH, 2, C_WIDTH), jnp.float32, minval=0.9, maxval=0.999)
    p = u ** (1.0 / RG_C)
    lam = jnp.log(p) - jnp.log1p(-p)
    return {
        'x_prompt': nrm(ks[0], (BATCH, SEQ, D_MODEL), 1.0),
        'x_sample': nrm(ks[1], (DEC_BATCH, DEC_SEQ, D_MODEL), 1.0),
        'norm_mix_w': gain(ks[2], (DEPTH, D_MODEL)),
        'w_in': nrm(ks[3], (DEPTH, D_MODEL, N_IN), D_MODEL ** -0.5),
        'hgrn_lb_logits': nrm(ks[4], (DEPTH, 2, A_KEY), 0.5),
        'hgrn_norm_w': gain(ks[5], (DEPTH, A_HEAD_V)),
        'gla_up_w': nrm(ks[6], (DEPTH, 2, GLA_RANK, B_KEY), GLA_RANK ** -0.5),
        'gla_up_b': nrm(ks[7], (DEPTH, 2, B_KEY), 0.1),
        'gla_norm_w': gain(ks[8], (DEPTH, B_HEAD_V)),
        'c_conv_w': nrm(ks[9], (DEPTH, C_CONV, C_WIDTH), C_CONV ** -0.5),
        'c_conv_b': nrm(ks[10], (DEPTH, C_WIDTH), 0.02),
        'rglru_wa': nrm(ks[11], (DEPTH, 2, C_BLOCKS, C_BLOCK, C_BLOCK), C_BLOCK ** -0.5),
        'rglru_ba': nrm(ks[12], (DEPTH, 2, C_WIDTH), 0.1),
        'rglru_wx': nrm(ks[13], (DEPTH, 2, C_BLOCKS, C_BLOCK, C_BLOCK), C_BLOCK ** -0.5),
        'rglru_bx': nrm(ks[14], (DEPTH, 2, C_WIDTH), 0.1),
        'rglru_lam': lam,
        'w_branch': nrm(ks[16], (DEPTH, N_BRANCH, BRANCH_WIDTH, D_MODEL), BRANCH_WIDTH ** -0.5),
        'w_out': nrm(ks[17], (DEPTH, D_MODEL, D_MODEL), D_MODEL ** -0.5),
        'norm_ffn_w': gain(ks[18], (DEPTH, D_MODEL)),
        'ffn_up': nrm(ks[19], (DEPTH, D_MODEL, 2 * D_FF), D_MODEL ** -0.5),
        'ffn_conv_w': nrm(ks[20], (DEPTH, FFN_CONV, 2 * D_FF), FFN_CONV ** -0.5),
        'ffn_conv_b': nrm(ks[21], (DEPTH, 2 * D_FF), 0.02),
        'ffn_down': nrm(ks[22], (DEPTH, D_FF, D_MODEL), D_FF ** -0.5),
        'final_norm_w': gain(ks[23], (D_MODEL,)),
    }


def reference(x_prompt, x_sample, norm_mix_w, w_in, hgrn_lb_logits, hgrn_norm_w, gla_up_w, gla_up_b,
              gla_norm_w, c_conv_w, c_conv_b, rglru_wa, rglru_ba, rglru_wx, rglru_bx, rglru_lam,
              w_branch, w_out, norm_ffn_w, ffn_up, ffn_conv_w, ffn_conv_b, ffn_down, final_norm_w):
    run = functools.partial(
        trunk, norm_mix_w=norm_mix_w, w_in=w_in, hgrn_lb_logits=hgrn_lb_logits, hgrn_norm_w=hgrn_norm_w,
        gla_up_w=gla_up_w, gla_up_b=gla_up_b, gla_norm_w=gla_norm_w, c_conv_w=c_conv_w, c_conv_b=c_conv_b,
        rglru_wa=rglru_wa, rglru_ba=rglru_ba, rglru_wx=rglru_wx, rglru_bx=rglru_bx, rglru_lam=rglru_lam,
        w_branch=w_branch, w_out=w_out, norm_ffn_w=norm_ffn_w, ffn_up=ffn_up, ffn_conv_w=ffn_conv_w,
        ffn_conv_b=ffn_conv_b, ffn_down=ffn_down, final_norm_w=final_norm_w)
    y_prompt = run(x_prompt)
    y_sample = run(x_sample)
    return (y_prompt, y_sample)
```

```python
import functools
import math

import jax
import jax.numpy as jnp
from jax import lax
from jax.experimental import pallas as pl
from jax.experimental.pallas import tpu as pltpu

F32 = jnp.float32
BF16 = jnp.bfloat16

D_MODEL = 1024
SEQ = 8192
DEPTH = 4
WIDTH = 512
HEAD_V = 128
A_KEY = 512
B_KEY = 256
GLA_RANK = 16
GLA_NORMALIZER = 16.0
C_BLOCKS = 8
C_BLOCK = 64
C_CONV = 4
RG_C = 8.0
D_FF = 2816
FFN_CONV = 3
EPS = 1e-6

LANES = 128
SUBLANES = 8
VMEM_LIMIT = 56 * 1024 * 1024

N_PROJ = 8320
COL_QA, COL_ZF, COL_ZB, COL_IA, COL_OGA = 0, 512, 1024, 1536, 2048
COL_QB, COL_KB, COL_VB, COL_OGB = 2560, 2816, 3072, 3584
COL_XC, COL_YC = 4096, 4608
COL_GA, COL_GB, COL_GC = 5120, 6144, 7168
COL_LR = 8192

SUB = 16
CHUNK = 64
N_SUB = CHUNK // SUB
T_MIX = 512
T_PROJ = 1024
N_PROJ_TILE = 1664
T_MERGE = 256
T_FFN = 1024
F_TILE = 256


def _sigmoid(x):
    return 1.0 / (1.0 + jnp.exp(-x))


def _gelu_tanh(x):
    return 0.5 * x * (1.0 + jnp.tanh(math.sqrt(2.0 / math.pi) * (x + 0.044715 * (x * x * x))))


def _rms(x, w):
    ms = jnp.mean(x * x, axis=-1, keepdims=True)
    return x * lax.rsqrt(ms + EPS) * w


def _proj_kernel(x_ref, nw_ref, w_ref, o_ref, h_ref):
    @pl.when(pl.program_id(1) == 0)
    def _():
        h_ref[...] = _rms(x_ref[...], nw_ref[...]).astype(BF16)

    o_ref[...] = jnp.dot(h_ref[...], w_ref[...], preferred_element_type=F32)


def _proj(x, nw, w):
    n_tok = x.shape[0]
    return pl.pallas_call(
        _proj_kernel,
        grid=(n_tok // T_PROJ, N_PROJ // N_PROJ_TILE),
        in_specs=[
            pl.BlockSpec((T_PROJ, D_MODEL), lambda i, j: (i, 0)),
            pl.BlockSpec((1, D_MODEL), lambda i, j: (0, 0)),
            pl.BlockSpec((D_MODEL, N_PROJ_TILE), lambda i, j: (0, j)),
        ],
        out_specs=pl.BlockSpec((T_PROJ, N_PROJ_TILE), lambda i, j: (i, j)),
        out_shape=jax.ShapeDtypeStruct((n_tok, N_PROJ), F32),
        scratch_shapes=[pltpu.VMEM((T_PROJ, D_MODEL), BF16)],
        compiler_params=pltpu.CompilerParams(
            dimension_semantics=("arbitrary", "arbitrary"), vmem_limit_bytes=VMEM_LIMIT),
        name="proj",
    )(x, nw, w)


def _tri_consts(reverse):
    i = lax.broadcasted_iota(jnp.int32, (CHUNK, CHUNK), 0)
    j = lax.broadcasted_iota(jnp.int32, (CHUNK, CHUNK), 1)
    causal = (j >= i) if reverse else (j <= i)
    same_sub = (i // SUB) == (j // SUB)
    tri = jnp.where(causal, 1.0, 0.0).astype(BF16)
    tri_sub = jnp.where(causal & same_sub, 1.0, 0.0).astype(BF16)
    return jnp.concatenate([tri, tri_sub], axis=0), causal


def _chunk_step(q, k, lg, v_list, st_list, head_masks, cum_mat, causal, reverse):
    n_heads = len(v_list)
    lg_hi = lg.astype(BF16)
    lg_lo = (lg - lg_hi.astype(F32)).astype(BF16)
    cum = jnp.dot(cum_mat, jnp.concatenate([lg_hi, lg_lo], axis=1), preferred_element_type=F32)
    b = cum[:CHUNK, :LANES] + cum[:CHUNK, LANES:]
    b_loc = cum[CHUNK:, :LANES] + cum[CHUNK:, LANES:]
    r = b - b_loc
    q_loc = q * jnp.exp(b_loc)
    k_inv = k * jnp.exp(-b_loc)
    q_dec = (q_loc * jnp.exp(r)).astype(BF16)
    last = 0 if reverse else CHUNK - 1
    b_last = b[last:last + 1, :]
    k_tail = (k * jnp.exp(b_last - b)).astype(BF16)
    e_last = jnp.exp(b_last)

    q_loc_bf = q_loc.astype(BF16)
    k_groups = []
    q_groups = []
    for s in range(N_SUB):
        lo, hi = (s * SUB, CHUNK) if reverse else (0, (s + 1) * SUB)
        r_s = r[s * SUB:s * SUB + 1, :]
        kt = (k_inv[lo:hi] * jnp.exp(r_s - r[lo:hi])).astype(BF16)
        pieces = []
        if lo > 0:
            pieces.append(jnp.zeros((lo, LANES), BF16))
        pieces.append(kt)
        if hi < CHUNK:
            pieces.append(jnp.zeros((CHUNK - hi, LANES), BF16))
        k_groups.append(jnp.concatenate(pieces, axis=0) if len(pieces) > 1 else kt)
        qp = []
        if s > 0:
            qp.append(jnp.zeros((s * SUB, LANES), BF16))
        qp.append(q_loc_bf[s * SUB:(s + 1) * SUB])
        if s < N_SUB - 1:
            qp.append(jnp.zeros((CHUNK - (s + 1) * SUB, LANES), BF16))
        q_groups.append(jnp.concatenate(qp, axis=0))
    k_wide = jnp.concatenate(k_groups, axis=1)
    q_wide = jnp.concatenate(q_groups, axis=1)
    if head_masks[0] is None:
        q_stack = q_wide
    else:
        zero = jnp.zeros_like(q_wide)
        q_stack = jnp.concatenate([jnp.where(m[0], q_wide, zero) for m in head_masks], axis=0)
    att = lax.dot_general(q_stack, k_wide, (((1,), (1,)), ((), ())), preferred_element_type=F32)

    outs, new_states = [], []
    for h in range(n_heads):
        att_h = jnp.where(causal, att[h * CHUNK:(h + 1) * CHUNK], 0.0).astype(BF16)
        qd = q_dec if head_masks[h] is None else jnp.where(head_masks[h][1], q_dec, jnp.zeros_like(q_dec))
        st = st_list[h]
        o = jnp.dot(att_h, v_list[h], preferred_element_type=F32)
        o = o + lax.dot_general(qd, st.astype(BF16), (((1,), (1,)), ((), ())),
                                preferred_element_type=F32)
        upd = lax.dot_general(v_list[h], k_tail, (((0,), (0,)), ((), ())), preferred_element_type=F32)
        outs.append(o)
        new_states.append(st * e_last + upd)
    return outs, new_states


def _gla_kernel(*refs, mode, n_groups, heads_per_group):
    if mode == "hgrn":
        (qf_ref, zf_ref, vf_ref, qb_ref, zb_ref, vb_ref, lb_ref, of_ref, ob_ref, st_ref) = refs
    else:
        (qf_ref, kf_ref, vf_ref, lrf_ref, qb_ref, kb_ref, vb_ref, lrb_ref, upw_ref, upb_ref,
         of_ref, ob_ref, st_ref, lg_ref) = refs
    n_heads = n_groups * heads_per_group

    @pl.when(pl.program_id(1) == 0)
    def _():
        st_ref[...] = jnp.zeros_like(st_ref)

    if mode == "gla":
        for d, lr_ref in enumerate((lrf_ref, lrb_ref)):
            pre = jnp.dot(lr_ref[...].astype(BF16), upw_ref[d], preferred_element_type=F32) + upb_ref[d]
            log_sig = jnp.minimum(pre, 0.0) - jnp.log(1.0 + jnp.exp(-jnp.abs(pre)))
            lg_ref[d] = log_sig / GLA_NORMALIZER

    consts = [_tri_consts(False), _tri_consts(True)]
    if heads_per_group == 1:
        head_masks = [None]
    else:
        width = LANES // heads_per_group
        lane_w = lax.broadcasted_iota(jnp.int32, (CHUNK, N_SUB * LANES), 1) % LANES
        lane = lax.broadcasted_iota(jnp.int32, (CHUNK, LANES), 1)
        head_masks = [tuple((ln >= h * width) & (ln < (h + 1) * width) for ln in (lane_w, lane))
                      for h in range(heads_per_group)]
    q_scale = (LANES // heads_per_group) ** -0.5
    n_chunks = T_MIX // CHUNK

    def body(c, carry):
        for d in range(2):
            reverse = d == 1
            cc = (n_chunks - 1 - c) if reverse else c
            rows = pl.ds(pl.multiple_of(cc * CHUNK, CHUNK), CHUNK)
            q_ref, v_ref, o_ref = (qb_ref, vb_ref, ob_ref) if reverse else (qf_ref, vf_ref, of_ref)
            cum_mat, causal = consts[d]
            for g in range(n_groups):
                cols = slice(g * LANES, (g + 1) * LANES)
                q = q_ref[rows, cols]
                if mode == "hgrn":
                    z = (zb_ref if reverse else zf_ref)[rows, cols]
                    lb = lb_ref[d:d + 1, cols]
                    y = lb + (1.0 - lb) * _sigmoid(z)
                    lg = jnp.log(y)
                    k = 1.0 - y
                    q = q * _sigmoid(q) * q_scale
                else:
                    k = (kb_ref if reverse else kf_ref)[rows, cols]
                    lg = lg_ref[d, rows, cols]
                    q = q * q_scale
                hs = [g * heads_per_group + h for h in range(heads_per_group)]
                v_list = [v_ref[rows, h * HEAD_V:(h + 1) * HEAD_V].astype(BF16) for h in hs]
                st_list = [st_ref[d * n_heads + h] for h in hs]
                outs, new_states = _chunk_step(q, k, lg, v_list, st_list, head_masks, cum_mat, causal,
                                               reverse)
                for h, o, st in zip(hs, outs, new_states):
                    o_ref[rows, h * HEAD_V:(h + 1) * HEAD_V] = o
                    st_ref[d * n_heads + h] = st
        return carry

    lax.fori_loop(0, n_chunks, body, 0)


def _bidir_gla(p, mode, extra):
    n_tok = p.shape[0]
    n_seq = n_tok // SEQ
    nb = SEQ // T_MIX
    fwd = lambda s, i: s * nb + i
    bwd = lambda s, i: s * nb + (nb - 1 - i)

    def col(width, offset, blk):
        return pl.BlockSpec((T_MIX, width), lambda s, i: (blk(s, i), offset // width))

    def whole(a):
        return pl.BlockSpec(a.shape, lambda s, i: (0,) * a.ndim)

    if mode == "hgrn":
        (lb,) = extra
        in_specs = [col(WIDTH, COL_QA, fwd), col(WIDTH, COL_ZF, fwd), col(WIDTH, COL_IA, fwd),
                    col(WIDTH, COL_QA, bwd), col(WIDTH, COL_ZB, bwd), col(WIDTH, COL_IA, bwd),
                    whole(lb)]
        args = [p] * 6 + [lb]
        n_groups, heads_per_group = 4, 1
        scratch = [pltpu.VMEM((8, HEAD_V, LANES), F32)]
    else:
        upw, upb = extra
        in_specs = [col(B_KEY, COL_QB, fwd), col(B_KEY, COL_KB, fwd), col(WIDTH, COL_VB, fwd),
                    col(LANES, COL_LR, fwd),
                    col(B_KEY, COL_QB, bwd), col(B_KEY, COL_KB, bwd), col(WIDTH, COL_VB, bwd),
                    col(LANES, COL_LR, bwd), whole(upw), whole(upb)]
        args = [p] * 8 + [upw, upb]
        n_groups, heads_per_group = 2, 2
        scratch = [pltpu.VMEM((8, HEAD_V, LANES), F32), pltpu.VMEM((2, T_MIX, B_KEY), F32)]
    out_spec_f = pl.BlockSpec((T_MIX, WIDTH), lambda s, i: (fwd(s, i), 0))
    out_spec_b = pl.BlockSpec((T_MIX, WIDTH), lambda s, i: (bwd(s, i), 0))
    return pl.pallas_call(
        functools.partial(_gla_kernel, mode=mode, n_groups=n_groups, heads_per_group=heads_per_group),
        grid=(n_seq, nb),
        in_specs=in_specs,
        out_specs=[out_spec_f, out_spec_b],
        out_shape=[jax.ShapeDtypeStruct((n_tok, WIDTH), F32)] * 2,
        scratch_shapes=scratch,
        compiler_params=pltpu.CompilerParams(
            dimension_semantics=("arbitrary", "arbitrary"), vmem_limit_bytes=VMEM_LIMIT),
        name="mix_" + mode,
    )(*args)


def _rglru_kernel(xf_ref, xfp_ref, xfn_ref, xb_ref, xbp_ref, xbn_ref, cw_ref, cb_ref, gw_ref, gb_ref,
                  lam_ref, of_ref, ob_ref, ext_ref, a_ref, u_ref, carry_ref):
    i = pl.program_id(1)
    nb = pl.num_programs(1)

    @pl.when(i == 0)
    def _():
        carry_ref[...] = jnp.zeros_like(carry_ref)

    row = lax.broadcasted_iota(jnp.int32, (T_MIX, WIDTH), 0) % SUBLANES
    n_rows = T_MIX // SUBLANES

    for d in range(2):
        reverse = d == 1
        x_ref, xp_ref, xn_ref, o_ref = (xb_ref, xbp_ref, xbn_ref, ob_ref) if reverse else (
            xf_ref, xfp_ref, xfn_ref, of_ref)
        blk = (nb - 1 - i) if reverse else i
        ext_ref[0:SUBLANES, :] = jnp.where(blk == 0, 0.0, xp_ref[...])
        ext_ref[SUBLANES:SUBLANES + T_MIX, :] = x_ref[...]
        ext_ref[SUBLANES + T_MIX:, :] = jnp.where(blk == nb - 1, 0.0, xn_ref[...])
        xc = cb_ref[...]
        for j in range(C_CONV):
            off = SUBLANES + j - C_CONV // 2
            xc = xc + ext_ref[off:off + T_MIX, :] * cw_ref[j:j + 1, :]
        xc_bf = xc.astype(BF16)
        gates = []
        for g in range(2):
            halves = [jnp.dot(xc_bf[:, hh * 256:(hh + 1) * 256], gw_ref[d, g, hh],
                              preferred_element_type=F32) for hh in range(2)]
            gates.append(_sigmoid(jnp.concatenate(halves, axis=1) + gb_ref[d, g:g + 1, :]))
        r_gate, i_gate = gates
        lam = lam_ref[d:d + 1, :]
        softplus_neg_lam = jnp.maximum(-lam, 0.0) + jnp.log(1.0 + jnp.exp(-jnp.abs(lam)))
        log_a = -RG_C * r_gate * softplus_neg_lam
        a = jnp.exp(log_a)
        u = jnp.sqrt(-jnp.tanh(log_a) * (1.0 + a * a)) * (i_gate * xc)
        for step in (1, 2, 4):
            shift = (T_MIX - step) if reverse else step
            a_sh = pltpu.roll(a, shift, 0)
            u_sh = pltpu.roll(u, shift, 0)
            keep = (row <= SUBLANES - 1 - step) if reverse else (row >= step)
            u = jnp.where(keep, a * u_sh + u, u)
            a = jnp.where(keep, a * a_sh, a)
        a_ref[...] = a
        u_ref[...] = u

        def body(n, h_prev):
            g = (n_rows - 1 - n) if reverse else n
            rows = pl.ds(pl.multiple_of(g * SUBLANES, SUBLANES), SUBLANES)
            h = a_ref[rows, :] * h_prev + u_ref[rows, :]
            o_ref[rows, :] = h
            edge = h[0:1, :] if reverse else h[SUBLANES - 1:SUBLANES, :]
            return jnp.broadcast_to(edge, (SUBLANES, WIDTH))

        carry_ref[d] = lax.fori_loop(0, n_rows, body, carry_ref[d], unroll=4)


def _rglru(p, cw, cb, gw, gb, lam):
    n_tok = p.shape[0]
    n_seq = n_tok // SEQ
    nb = SEQ // T_MIX
    halo = T_MIX // SUBLANES
    last_halo = n_tok // SUBLANES - 1
    xcol = COL_XC // WIDTH
    fwd = lambda s, i: s * nb + i
    bwd = lambda s, i: s * nb + (nb - 1 - i)

    def specs(blk):
        return [
            pl.BlockSpec((T_MIX, WIDTH), lambda s, i: (blk(s, i), xcol)),
            pl.BlockSpec((SUBLANES, WIDTH), lambda s, i: (jnp.maximum(blk(s, i) * halo - 1, 0), xcol)),
            pl.BlockSpec((SUBLANES, WIDTH),
                         lambda s, i: (jnp.minimum((blk(s, i) + 1) * halo, last_halo), xcol)),
        ]

    def whole(a):
        return pl.BlockSpec(a.shape, lambda s, i: (0,) * a.ndim)

    return pl.pallas_call(
        _rglru_kernel,
        grid=(n_seq, nb),
        in_specs=specs(fwd) + specs(bwd) + [whole(cw), whole(cb), whole(gw), whole(gb), whole(lam)],
        out_specs=[pl.BlockSpec((T_MIX, WIDTH), lambda s, i: (fwd(s, i), 0)),
                   pl.BlockSpec((T_MIX, WIDTH), lambda s, i: (bwd(s, i), 0))],
        out_shape=[jax.ShapeDtypeStruct((n_tok, WIDTH), F32)] * 2,
        scratch_shapes=[pltpu.VMEM((T_MIX + 2 * SUBLANES, WIDTH), F32),
                        pltpu.VMEM((T_MIX, WIDTH), F32),
                        pltpu.VMEM((T_MIX, WIDTH), F32),
                        pltpu.VMEM((2, SUBLANES, WIDTH), F32)],
        compiler_params=pltpu.CompilerParams(
            dimension_semantics=("arbitrary", "arbitrary"), vmem_limit_bytes=VMEM_LIMIT),
        name="mix_rglru",
    )(p, p, p, p, p, p, cw, cb, gw, gb, lam)


def _head_norm(o, w):
    parts = [_rms(o[:, h * HEAD_V:(h + 1) * HEAD_V], w) for h in range(WIDTH // HEAD_V)]
    return jnp.concatenate(parts, axis=1)


def _merge_kernel(x_ref, oaf_ref, oab_ref, oga_ref, obf_ref, obb_ref, ogb_ref, hcf_ref, hcb_ref, yc_ref,
                  ga_ref, gb_ref, gc_ref, nwa_ref, nwb_ref, wbr_ref, wout_ref, o_ref):
    oga = oga_ref[...]
    ogb = ogb_ref[...]
    o_a = _head_norm(oaf_ref[...] + oab_ref[...], nwa_ref[...]) * (oga * _sigmoid(oga))
    o_b = _head_norm(obf_ref[...] + obb_ref[...], nwb_ref[...]) * (ogb * _sigmoid(ogb))
    o_c = (hcf_ref[...] + hcb_ref[...]) * _gelu_tanh(yc_ref[...])
    m = _sigmoid(ga_ref[...]) * jnp.dot(o_a.astype(BF16), wbr_ref[0], preferred_element_type=F32)
    m = m + _sigmoid(gb_ref[...]) * jnp.dot(o_b.astype(BF16), wbr_ref[1], preferred_element_type=F32)
    m = m + _sigmoid(gc_ref[...]) * jnp.dot(o_c.astype(BF16), wbr_ref[2], preferred_element_type=F32)
    o_ref[...] = x_ref[...] + jnp.dot(m.astype(BF16), wout_ref[...], preferred_element_type=F32)


def _merge(x, p, oaf, oab, obf, obb, hcf, hcb, nwa, nwb, wbr, wout):
    n_tok = x.shape[0]
    tok = lambda width, offset: pl.BlockSpec((T_MERGE, width), lambda i: (i, offset // width))

    def whole(a):
        return pl.BlockSpec(a.shape, lambda i: (0,) * a.ndim)

    return pl.pallas_call(
        _merge_kernel,
        grid=(n_tok // T_MERGE,),
        in_specs=[tok(D_MODEL, 0), tok(WIDTH, 0), tok(WIDTH, 0), tok(WIDTH, COL_OGA),
                  tok(WIDTH, 0), tok(WIDTH, 0), tok(WIDTH, COL_OGB),
                  tok(WIDTH, 0), tok(WIDTH, 0), tok(WIDTH, COL_YC),
                  tok(D_MODEL, COL_GA), tok(D_MODEL, COL_GB), tok(D_MODEL, COL_GC),
                  whole(nwa), whole(nwb), whole(wbr), whole(wout)],
        out_specs=tok(D_MODEL, 0),
        out_shape=jax.ShapeDtypeStruct((n_tok, D_MODEL), F32),
        compiler_params=pltpu.CompilerParams(
            dimension_semantics=("arbitrary",), vmem_limit_bytes=VMEM_LIMIT),
        name="merge",
    )(x, oaf, oab, p, obf, obb, p, hcf, hcb, p, p, p, p, nwa, nwb, wbr, wout)


def _ffn_kernel(x_ref, xp_ref, xn_ref, nw_ref, wg_ref, wv_ref, cwg_ref, cwv_ref, cbg_ref, cbv_ref,
                wd_ref, fw_ref, o_ref, h_ref, ug_ref, uv_ref, acc_ref, *, final_norm):
    i = pl.program_id(0)
    j = pl.program_id(1)
    blocks_per_seq = SEQ // T_FFN

    @pl.when(j == 0)
    def _():
        nw = nw_ref[...]
        first = (i % blocks_per_seq) == 0
        last = (i % blocks_per_seq) == blocks_per_seq - 1
        h_ref[0:SUBLANES, :] = jnp.where(first, 0.0, _rms(xp_ref[...], nw)).astype(BF16)
        h_ref[SUBLANES:SUBLANES + T_FFN, :] = _rms(x_ref[...], nw).astype(BF16)
        h_ref[SUBLANES + T_FFN:, :] = jnp.where(last, 0.0, _rms(xn_ref[...], nw)).astype(BF16)
        acc_ref[...] = x_ref[...]

    h = h_ref[...]
    ug_ref[...] = jnp.dot(h, wg_ref[...], preferred_element_type=F32)
    uv_ref[...] = jnp.dot(h, wv_ref[...], preferred_element_type=F32)

    def conv(u_ref, cw_ref, cb_ref):
        out = cb_ref[...]
        for t in range(FFN_CONV):
            off = SUBLANES + t - FFN_CONV // 2
            out = out + u_ref[off:off + T_FFN, :] * cw_ref[t:t + 1, :]
        return out

    act = _gelu_tanh(conv(ug_ref, cwg_ref, cbg_ref)) * conv(uv_ref, cwv_ref, cbv_ref)
    acc_ref[...] += jnp.dot(act.astype(BF16), wd_ref[...], preferred_element_type=F32)

    @pl.when(j == pl.num_programs(1) - 1)
    def _():
        y = acc_ref[...]
        o_ref[...] = _rms(y, fw_ref[...]) if final_norm else y


def _ffn(x, nw, w_up, cw, cb, w_down, fw, final_norm):
    n_tok = x.shape[0]
    halo = T_FFN // SUBLANES
    last_halo = n_tok // SUBLANES - 1
    n_f = D_FF // F_TILE
    return pl.pallas_call(
        functools.partial(_ffn_kernel, final_norm=final_norm),
        grid=(n_tok // T_FFN, n_f),
        in_specs=[
            pl.BlockSpec((T_FFN, D_MODEL), lambda i, j: (i, 0)),
            pl.BlockSpec((SUBLANES, D_MODEL), lambda i, j: (jnp.maximum(i * halo - 1, 0), 0)),
            pl.BlockSpec((SUBLANES, D_MODEL), lambda i, j: (jnp.minimum((i + 1) * halo, last_halo), 0)),
            pl.BlockSpec((1, D_MODEL), lambda i, j: (0, 0)),
            pl.BlockSpec((D_MODEL, F_TILE), lambda i, j: (0, j)),
            pl.BlockSpec((D_MODEL, F_TILE), lambda i, j: (0, n_f + j)),
            pl.BlockSpec((FFN_CONV, F_TILE), lambda i, j: (0, j)),
            pl.BlockSpec((FFN_CONV, F_TILE), lambda i, j: (0, n_f + j)),
            pl.BlockSpec((1, F_TILE), lambda i, j: (0, j)),
            pl.BlockSpec((1, F_TILE), lambda i, j: (0, n_f + j)),
            pl.BlockSpec((F_TILE, D_MODEL), lambda i, j: (j, 0)),
            pl.BlockSpec((1, D_MODEL), lambda i, j: (0, 0)),
        ],
        out_specs=pl.BlockSpec((T_FFN, D_MODEL), lambda i, j: (i, 0)),
        out_shape=jax.ShapeDtypeStruct((n_tok, D_MODEL), F32),
        scratch_shapes=[pltpu.VMEM((T_FFN + 2 * SUBLANES, D_MODEL), BF16),
                        pltpu.VMEM((T_FFN + 2 * SUBLANES, F_TILE), F32),
                        pltpu.VMEM((T_FFN + 2 * SUBLANES, F_TILE), F32),
                        pltpu.VMEM((T_FFN, D_MODEL), F32)],
        compiler_params=pltpu.CompilerParams(
            dimension_semantics=("arbitrary", "arbitrary"), vmem_limit_bytes=VMEM_LIMIT),
        name="ffn",
    )(x, x, x, nw, w_up, w_up, cw, cw, cb, cb, w_down, fw)


def _layout_w_in(w_in_l):
    (q_a, zf_a, zb_a, i_a, og_a, q_b, k_b, v_b, og_b, lrf, lrb, x_c, y_c, g_a, g_b, g_c) = jnp.split(
        w_in_l, [512, 1024, 1536, 2048, 2560, 2816, 3072, 3584, 4096, 4112, 4128, 4640, 5152, 6176, 7200],
        axis=1)
    pad = jnp.zeros((D_MODEL, LANES - 2 * GLA_RANK), w_in_l.dtype)
    return jnp.concatenate([q_a, zf_a, zb_a, i_a, og_a, q_b, k_b, v_b, og_b, x_c, y_c, g_a, g_b, g_c,
                            lrf, lrb, pad], axis=1).astype(BF16)


def _layout_gla_up(up_w_l):
    out = jnp.zeros((2, LANES, B_KEY), F32)
    out = out.at[0, 0:GLA_RANK].set(up_w_l[0])
    out = out.at[1, GLA_RANK:2 * GLA_RANK].set(up_w_l[1])
    return out.astype(BF16)


def _layout_rglru_gates(wa_l, wx_l):
    def tiles(w):
        w = w.reshape(2, 2, 4, C_BLOCK, C_BLOCK)
        eye = jnp.eye(4, dtype=w.dtype)
        t = jnp.einsum('dhbkj,bc->dhbkcj', w, eye)
        return t.reshape(2, 2, 4 * C_BLOCK, 4 * C_BLOCK)
    return jnp.stack([tiles(wa_l), tiles(wx_l)], axis=1).astype(BF16)


def kernel(x_prompt, x_sample, norm_mix_w, w_in, hgrn_lb_logits, hgrn_norm_w, gla_up_w, gla_up_b,
           gla_norm_w, c_conv_w, c_conv_b, rglru_wa, rglru_ba, rglru_wx, rglru_bx, rglru_lam,
           w_branch, w_out, norm_ffn_w, ffn_up, ffn_conv_w, ffn_conv_b, ffn_down, final_norm_w):
    n_prompt = x_prompt.shape[0] * x_prompt.shape[1]
    x = jnp.concatenate([x_prompt.reshape(-1, D_MODEL), x_sample.reshape(-1, D_MODEL)], axis=0)

    lb_all = jnp.cumsum(jax.nn.softmax(hgrn_lb_logits.astype(F32), axis=0), axis=0)
    lb_all = lb_all - lb_all[0]
    fw = final_norm_w.reshape(1, D_MODEL)

    for l in range(DEPTH):
        p = _proj(x, norm_mix_w[l].reshape(1, D_MODEL), _layout_w_in(w_in[l]))
        oaf, oab = _bidir_gla(p, "hgrn", (lb_all[l],))
        obf, obb = _bidir_gla(p, "gla", (_layout_gla_up(gla_up_w[l]), gla_up_b[l].reshape(2, 1, B_KEY)))
        hcf, hcb = _rglru(p, c_conv_w[l], c_conv_b[l].reshape(1, WIDTH),
                          _layout_rglru_gates(rglru_wa[l], rglru_wx[l]),
                          jnp.stack([rglru_ba[l], rglru_bx[l]], axis=1), rglru_lam[l])
        x = _merge(x, p, oaf, oab, obf, obb, hcf, hcb,
                   hgrn_norm_w[l].reshape(1, HEAD_V), gla_norm_w[l].reshape(1, HEAD_V),
                   w_branch[l].astype(BF16), w_out[l].astype(BF16))
        x = _ffn(x, norm_ffn_w[l].reshape(1, D_MODEL), ffn_up[l].astype(BF16), ffn_conv_w[l],
                 ffn_conv_b[l].reshape(1, 2 * D_FF), ffn_down[l].astype(BF16), fw,
                 final_norm=(l == DEPTH - 1))

    y_prompt = x[:n_prompt].reshape(x_prompt.shape)
    y_sample = x[n_prompt:].reshape(x_sample.shape)
    return (y_prompt, y_sample)
```

```python
import functools
import math

import jax
import jax.numpy as jnp
from jax import lax
from jax.experimental import pallas as pl
from jax.experimental.pallas import tpu as pltpu

F32 = jnp.float32
BF16 = jnp.bfloat16
P_DTYPE = BF16
O_DTYPE = BF16

D_MODEL = 1024
SEQ = 8192
DEPTH = 4
WIDTH = 512
HEAD_V = 128
A_KEY = 512
B_KEY = 256
GLA_RANK = 16
GLA_NORMALIZER = 16.0
C_BLOCKS = 8
C_BLOCK = 64
C_CONV = 4
RG_C = 8.0
D_FF = 2816
FFN_CONV = 3
EPS = 1e-6

LANES = 128
SUBLANES = 8
HALO = 16
VMEM_LIMIT = 56 * 1024 * 1024

N_PROJ = 8320
COL_QA, COL_ZF, COL_ZB, COL_IA, COL_OGA = 0, 512, 1024, 1536, 2048
COL_QB, COL_KB, COL_VB, COL_OGB = 2560, 2816, 3072, 3584
COL_XC, COL_YC = 4096, 4608
COL_GA, COL_GB, COL_GC = 5120, 6144, 7168
COL_LR = 8192

SUB = 16
CHUNK = 64
N_SUB = CHUNK // SUB
PHASE_LAG = 4
T_MIX = 512
T_PROJ = 1024
N_PROJ_TILE = 1664
T_MERGE = 512
T_FFN = 1024
T_FFN_SUB = 256
F_TILE = 256


def _sigmoid(x):
    return 1.0 / (1.0 + jnp.exp(-x))


def _sigmoid_tanh(x):
    return 0.5 + 0.5 * jnp.tanh(0.5 * x)


def _gelu_tanh(x):
    return 0.5 * x * (1.0 + jnp.tanh(math.sqrt(2.0 / math.pi) * (x + 0.044715 * (x * x * x))))


def _rms(x, w):
    ms = jnp.mean(x * x, axis=-1, keepdims=True)
    return x * lax.rsqrt(ms + EPS) * w


def _proj_kernel(x_ref, nw_ref, w_ref, o_ref, h_ref):
    @pl.when(pl.program_id(1) == 0)
    def _():
        h_ref[...] = _rms(x_ref[...], nw_ref[...]).astype(BF16)

    o_ref[...] = jnp.dot(h_ref[...], w_ref[...], preferred_element_type=F32).astype(o_ref.dtype)


def _proj(x, nw, w):
    n_tok = x.shape[0]
    return pl.pallas_call(
        _proj_kernel,
        grid=(n_tok // T_PROJ, N_PROJ // N_PROJ_TILE),
        in_specs=[
            pl.BlockSpec((T_PROJ, D_MODEL), lambda i, j: (i, 0)),
            pl.BlockSpec((1, D_MODEL), lambda i, j: (0, 0)),
            pl.BlockSpec((D_MODEL, N_PROJ_TILE), lambda i, j: (0, j)),
        ],
        out_specs=pl.BlockSpec((T_PROJ, N_PROJ_TILE), lambda i, j: (i, j)),
        out_shape=jax.ShapeDtypeStruct((n_tok, N_PROJ), P_DTYPE),
        scratch_shapes=[pltpu.VMEM((T_PROJ, D_MODEL), BF16)],
        compiler_params=pltpu.CompilerParams(
            dimension_semantics=("arbitrary", "arbitrary"), vmem_limit_bytes=VMEM_LIMIT),
        name="proj",
    )(x, nw, w)


def _tri_consts(reverse):
    i = lax.broadcasted_iota(jnp.int32, (CHUNK, CHUNK), 0)
    j = lax.broadcasted_iota(jnp.int32, (CHUNK, CHUNK), 1)
    causal = (j >= i) if reverse else (j <= i)
    same_sub = (i // SUB) == (j // SUB)
    tri = jnp.where(causal, 1.0, 0.0).astype(BF16)
    tri_sub = jnp.where(causal & same_sub, 1.0, 0.0).astype(BF16)
    return jnp.concatenate([tri, tri_sub], axis=0), causal


def _chunk_cumsum(lg, cum_mat):
    lg_hi = lg.astype(BF16)
    lg_lo = (lg - lg_hi.astype(F32)).astype(BF16)
    return jnp.dot(cum_mat, jnp.concatenate([lg_hi, lg_lo], axis=1), preferred_element_type=F32)


def _chunk_scores(cum, q, k, v_list, st_list, head_masks, reverse):
    n_heads = len(v_list)
    b = cum[:CHUNK, :LANES] + cum[:CHUNK, LANES:]
    b_loc = cum[CHUNK:, :LANES] + cum[CHUNK:, LANES:]
    r = b - b_loc
    q_loc = q * jnp.exp(b_loc)
    k_inv = k * jnp.exp(-b_loc)
    q_dec = (q_loc * jnp.exp(r)).astype(BF16)
    last = 0 if reverse else CHUNK - 1
    b_last = b[last:last + 1, :]
    k_tail = (k * jnp.exp(b_last - b)).astype(BF16)
    e_last = jnp.exp(b_last)

    q_loc_bf = q_loc.astype(BF16)
    k_groups = []
    q_groups = []
    for s in range(N_SUB):
        lo, hi = (s * SUB, CHUNK) if reverse else (0, (s + 1) * SUB)
        r_s = r[s * SUB:s * SUB + 1, :]
        kt = (k_inv[lo:hi] * jnp.exp(r_s - r[lo:hi])).astype(BF16)
        pieces = []
        if lo > 0:
            pieces.append(jnp.zeros((lo, LANES), BF16))
        pieces.append(kt)
        if hi < CHUNK:
            pieces.append(jnp.zeros((CHUNK - hi, LANES), BF16))
        k_groups.append(jnp.concatenate(pieces, axis=0) if len(pieces) > 1 else kt)
        qp = []
        if s > 0:
            qp.append(jnp.zeros((s * SUB, LANES), BF16))
        qp.append(q_loc_bf[s * SUB:(s + 1) * SUB])
        if s < N_SUB - 1:
            qp.append(jnp.zeros((CHUNK - (s + 1) * SUB, LANES), BF16))
        q_groups.append(jnp.concatenate(qp, axis=0))
    k_wide = jnp.concatenate(k_groups, axis=1)
    q_wide = jnp.concatenate(q_groups, axis=1)
    if head_masks[0] is None:
        q_stack = q_wide
    else:
        zero = jnp.zeros_like(q_wide)
        q_stack = jnp.concatenate([jnp.where(m[0], q_wide, zero) for m in head_masks], axis=0)
    att = lax.dot_general(q_stack, k_wide, (((1,), (1,)), ((), ())), preferred_element_type=F32)

    o_inter, upd = [], []
    for h in range(n_heads):
        qd = q_dec if head_masks[h] is None else jnp.where(head_masks[h][1], q_dec, jnp.zeros_like(q_dec))
        o_inter.append(lax.dot_general(qd, st_list[h].astype(BF16), (((1,), (1,)), ((), ())),
                                       preferred_element_type=F32))
        upd.append(lax.dot_general(v_list[h], k_tail, (((0,), (0,)), ((), ())),
                                   preferred_element_type=F32))
    return att, o_inter, upd, e_last


def _chunk_outputs(att, o_inter, upd, e_last, v_list, st_list, causal):
    outs, new_states = [], []
    for h in range(len(v_list)):
        att_h = jnp.where(causal, att[h * CHUNK:(h + 1) * CHUNK], 0.0).astype(BF16)
        outs.append(jnp.dot(att_h, v_list[h], preferred_element_type=F32) + o_inter[h])
        new_states.append(st_list[h] * e_last + upd[h])
    return outs, new_states


def _gla_kernel(*refs, mode, n_groups, heads_per_group):
    if mode == "hgrn":
        (qf_ref, zf_ref, vf_ref, qb_ref, zb_ref, vb_ref, lb_ref, of_ref, ob_ref, st_ref) = refs
    else:
        (qf_ref, kf_ref, vf_ref, lrf_ref, qb_ref, kb_ref, vb_ref, lrb_ref, upw_ref, upb_ref,
         of_ref, ob_ref, st_ref, lg_ref) = refs
    n_heads = n_groups * heads_per_group

    @pl.when(pl.program_id(1) == 0)
    def _():
        st_ref[...] = jnp.zeros_like(st_ref)

    if mode == "gla":
        for d, lr_ref in enumerate((lrf_ref, lrb_ref)):
            pre = jnp.dot(lr_ref[...].astype(BF16), upw_ref[d], preferred_element_type=F32) + upb_ref[d]
            log_sig = jnp.minimum(pre, 0.0) - jnp.log(1.0 + jnp.exp(-jnp.abs(pre)))
            lg_ref[d] = log_sig / GLA_NORMALIZER

    consts = [_tri_consts(False), _tri_consts(True)]
    if heads_per_group == 1:
        head_masks = [None]
    else:
        width = LANES // heads_per_group
        lane_w = lax.broadcasted_iota(jnp.int32, (CHUNK, N_SUB * LANES), 1) % LANES
        lane = lax.broadcasted_iota(jnp.int32, (CHUNK, LANES), 1)
        head_masks = [tuple((ln >= h * width) & (ln < (h + 1) * width) for ln in (lane_w, lane))
                      for h in range(heads_per_group)]
    q_scale = (LANES // heads_per_group) ** -0.5
    n_chunks = T_MIX // CHUNK

    chains = [(d, g) for g in range(n_groups) for d in range(2)]

    def body(c, carry):
        work = [dict() for _ in chains]

        def phase1(w, d, g):
            reverse = d == 1
            cc = (n_chunks - 1 - c) if reverse else c
            w["rows"] = rows = pl.ds(pl.multiple_of(cc * CHUNK, CHUNK), CHUNK)
            cols = slice(g * LANES, (g + 1) * LANES)
            q = (qb_ref if reverse else qf_ref)[rows, cols].astype(F32)
            if mode == "hgrn":
                z = (zb_ref if reverse else zf_ref)[rows, cols].astype(F32)
                lb = lb_ref[d:d + 1, cols]
                y = lb + (1.0 - lb) * _sigmoid(z)
                lg = jnp.log(y)
                w["k"] = 1.0 - y
                w["q"] = q * _sigmoid_tanh(q) * q_scale
            else:
                w["k"] = (kb_ref if reverse else kf_ref)[rows, cols].astype(F32)
                lg = lg_ref[d, rows, cols]
                w["q"] = q * q_scale
            w["cum"] = _chunk_cumsum(lg, consts[d][0])

        def phase2(w, d, g):
            v_ref = vb_ref if d == 1 else vf_ref
            w["hs"] = hs = [g * heads_per_group + h for h in range(heads_per_group)]
            w["v"] = [v_ref[w["rows"], h * HEAD_V:(h + 1) * HEAD_V].astype(BF16) for h in hs]
            w["st"] = [st_ref[d * n_heads + h] for h in hs]
            w["scores"] = _chunk_scores(w["cum"], w["q"], w["k"], w["v"], w["st"], head_masks, d == 1)

        def phase3(w, d, g):
            o_ref = ob_ref if d == 1 else of_ref
            outs, new_states = _chunk_outputs(*w["scores"], w["v"], w["st"], consts[d][1])
            for h, o, st in zip(w["hs"], outs, new_states):
                o_ref[w["rows"], h * HEAD_V:(h + 1) * HEAD_V] = o.astype(o_ref.dtype)
                st_ref[d * n_heads + h] = st

        phases = (phase1, phase2, phase3)
        for slot in range(len(chains) + PHASE_LAG * (len(phases) - 1)):
            for p, phase in enumerate(phases):
                n = slot - p * PHASE_LAG
                if 0 <= n < len(chains):
                    phase(work[n], *chains[n])
        return carry

    lax.fori_loop(0, n_chunks, body, 0, unroll=2)


def _bidir_gla(p, mode, extra):
    n_tok = p.shape[0]
    n_seq = n_tok // SEQ
    nb = SEQ // T_MIX
    fwd = lambda s, i: s * nb + i
    bwd = lambda s, i: s * nb + (nb - 1 - i)

    def col(width, offset, blk):
        return pl.BlockSpec((T_MIX, width), lambda s, i: (blk(s, i), offset // width))

    def whole(a):
        return pl.BlockSpec(a.shape, lambda s, i: (0,) * a.ndim)

    if mode == "hgrn":
        (lb,) = extra
        in_specs = [col(WIDTH, COL_QA, fwd), col(WIDTH, COL_ZF, fwd), col(WIDTH, COL_IA, fwd),
                    col(WIDTH, COL_QA, bwd), col(WIDTH, COL_ZB, bwd), col(WIDTH, COL_IA, bwd),
                    whole(lb)]
        args = [p] * 6 + [lb]
        n_groups, heads_per_group = 4, 1
        scratch = [pltpu.VMEM((8, HEAD_V, LANES), F32)]
    else:
        upw, upb = extra
        in_specs = [col(B_KEY, COL_QB, fwd), col(B_KEY, COL_KB, fwd), col(WIDTH, COL_VB, fwd),
                    col(LANES, COL_LR, fwd),
                    col(B_KEY, COL_QB, bwd), col(B_KEY, COL_KB, bwd), col(WIDTH, COL_VB, bwd),
                    col(LANES, COL_LR, bwd), whole(upw), whole(upb)]
        args = [p] * 8 + [upw, upb]
        n_groups, heads_per_group = 2, 2
        scratch = [pltpu.VMEM((8, HEAD_V, LANES), F32), pltpu.VMEM((2, T_MIX, B_KEY), F32)]
    out_spec_f = pl.BlockSpec((T_MIX, WIDTH), lambda s, i: (fwd(s, i), 0))
    out_spec_b = pl.BlockSpec((T_MIX, WIDTH), lambda s, i: (bwd(s, i), 0))
    return pl.pallas_call(
        functools.partial(_gla_kernel, mode=mode, n_groups=n_groups, heads_per_group=heads_per_group),
        grid=(n_seq, nb),
        in_specs=in_specs,
        out_specs=[out_spec_f, out_spec_b],
        out_shape=[jax.ShapeDtypeStruct((n_tok, WIDTH), O_DTYPE)] * 2,
        scratch_shapes=scratch,
        compiler_params=pltpu.CompilerParams(
            dimension_semantics=("arbitrary", "arbitrary"), vmem_limit_bytes=VMEM_LIMIT),
        name="mix_" + mode,
    )(*args)


def _rglru_kernel(xf_ref, xfp_ref, xfn_ref, xb_ref, xbp_ref, xbn_ref, cw_ref, cb_ref, gw_ref, gb_ref,
                  lam_ref, of_ref, ob_ref, ext_ref, a_ref, u_ref, carry_ref):
    i = pl.program_id(1)
    nb = pl.num_programs(1)

    @pl.when(i == 0)
    def _():
        carry_ref[...] = jnp.zeros_like(carry_ref)

    row = lax.broadcasted_iota(jnp.int32, (T_MIX, WIDTH), 0) % SUBLANES
    n_rows = T_MIX // SUBLANES

    for d in range(2):
        reverse = d == 1
        x_ref, xp_ref, xn_ref, o_ref = (xb_ref, xbp_ref, xbn_ref, ob_ref) if reverse else (
            xf_ref, xfp_ref, xfn_ref, of_ref)
        blk = (nb - 1 - i) if reverse else i
        ext_ref[0:SUBLANES, :] = jnp.where(blk == 0, 0.0, xp_ref[...].astype(F32)[HALO - SUBLANES:, :])
        ext_ref[SUBLANES:SUBLANES + T_MIX, :] = x_ref[...].astype(F32)
        ext_ref[SUBLANES + T_MIX:, :] = jnp.where(blk == nb - 1, 0.0, xn_ref[...].astype(F32)[:SUBLANES, :])
        xc = cb_ref[...]
        for j in range(C_CONV):
            off = SUBLANES + j - C_CONV // 2
            xc = xc + ext_ref[off:off + T_MIX, :] * cw_ref[j:j + 1, :]
        xc_bf = xc.astype(BF16)
        gates = []
        for g in range(2):
            halves = [jnp.dot(xc_bf[:, hh * 256:(hh + 1) * 256], gw_ref[d, g, hh],
                              preferred_element_type=F32) for hh in range(2)]
            gates.append(_sigmoid_tanh(jnp.concatenate(halves, axis=1) + gb_ref[d, g:g + 1, :]))
        r_gate, i_gate = gates
        lam = lam_ref[d:d + 1, :]
        softplus_neg_lam = jnp.maximum(-lam, 0.0) + jnp.log(1.0 + jnp.exp(-jnp.abs(lam)))
        log_a = -RG_C * r_gate * softplus_neg_lam
        a = jnp.exp(log_a)
        u = jnp.sqrt(-jnp.tanh(log_a) * (1.0 + a * a)) * (i_gate * xc)
        for step in (1, 2, 4):
            shift = (T_MIX - step) if reverse else step
            a_sh = pltpu.roll(a, shift, 0)
            u_sh = pltpu.roll(u, shift, 0)
            keep = (row <= SUBLANES - 1 - step) if reverse else (row >= step)
            u = jnp.where(keep, a * u_sh + u, u)
            a = jnp.where(keep, a * a_sh, a)
        a_ref[...] = a
        u_ref[...] = u

        def body(n, h_prev):
            g = (n_rows - 1 - n) if reverse else n
            rows = pl.ds(pl.multiple_of(g * SUBLANES, SUBLANES), SUBLANES)
            h = a_ref[rows, :] * h_prev + u_ref[rows, :]
            u_ref[rows, :] = h
            edge = h[0:1, :] if reverse else h[SUBLANES - 1:SUBLANES, :]
            return jnp.broadcast_to(edge, (SUBLANES, WIDTH))

        carry_ref[d] = lax.fori_loop(0, n_rows, body, carry_ref[d], unroll=4)
        o_ref[...] = u_ref[...].astype(o_ref.dtype)


def _rglru(p, cw, cb, gw, gb, lam):
    n_tok = p.shape[0]
    n_seq = n_tok // SEQ
    nb = SEQ // T_MIX
    halo = T_MIX // HALO
    last_halo = n_tok // HALO - 1
    xcol = COL_XC // WIDTH
    fwd = lambda s, i: s * nb + i
    bwd = lambda s, i: s * nb + (nb - 1 - i)

    def specs(blk):
        return [
            pl.BlockSpec((T_MIX, WIDTH), lambda s, i: (blk(s, i), xcol)),
            pl.BlockSpec((HALO, WIDTH), lambda s, i: (jnp.maximum(blk(s, i) * halo - 1, 0), xcol)),
            pl.BlockSpec((HALO, WIDTH),
                         lambda s, i: (jnp.minimum((blk(s, i) + 1) * halo, last_halo), xcol)),
        ]

    def whole(a):
        return pl.BlockSpec(a.shape, lambda s, i: (0,) * a.ndim)

    return pl.pallas_call(
        _rglru_kernel,
        grid=(n_seq, nb),
        in_specs=specs(fwd) + specs(bwd) + [whole(cw), whole(cb), whole(gw), whole(gb), whole(lam)],
        out_specs=[pl.BlockSpec((T_MIX, WIDTH), lambda s, i: (fwd(s, i), 0)),
                   pl.BlockSpec((T_MIX, WIDTH), lambda s, i: (bwd(s, i), 0))],
        out_shape=[jax.ShapeDtypeStruct((n_tok, WIDTH), O_DTYPE)] * 2,
        scratch_shapes=[pltpu.VMEM((T_MIX + 2 * SUBLANES, WIDTH), F32),
                        pltpu.VMEM((T_MIX, WIDTH), F32),
                        pltpu.VMEM((T_MIX, WIDTH), F32),
                        pltpu.VMEM((2, SUBLANES, WIDTH), F32)],
        compiler_params=pltpu.CompilerParams(
            dimension_semantics=("arbitrary", "arbitrary"), vmem_limit_bytes=VMEM_LIMIT),
        name="mix_rglru",
    )(p, p, p, p, p, p, cw, cb, gw, gb, lam)


def _head_norm(o, w):
    parts = [_rms(o[:, h * HEAD_V:(h + 1) * HEAD_V], w) for h in range(WIDTH // HEAD_V)]
    return jnp.concatenate(parts, axis=1)


def _merge_kernel(x_ref, oaf_ref, oab_ref, oga_ref, obf_ref, obb_ref, ogb_ref, hcf_ref, hcb_ref, yc_ref,
                  ga_ref, gb_ref, gc_ref, nwa_ref, nwb_ref, wbr_ref, wout_ref, o_ref):
    ld = lambda ref: ref[...].astype(F32)
    oga = ld(oga_ref)
    ogb = ld(ogb_ref)
    o_a = _head_norm(ld(oaf_ref) + ld(oab_ref), nwa_ref[...]) * (oga * _sigmoid_tanh(oga))
    o_b = _head_norm(ld(obf_ref) + ld(obb_ref), nwb_ref[...]) * (ogb * _sigmoid_tanh(ogb))
    o_c = (ld(hcf_ref) + ld(hcb_ref)) * _gelu_tanh(ld(yc_ref))
    m = _sigmoid_tanh(ld(ga_ref)) * jnp.dot(o_a.astype(BF16), wbr_ref[0], preferred_element_type=F32)
    m = m + _sigmoid_tanh(ld(gb_ref)) * jnp.dot(o_b.astype(BF16), wbr_ref[1], preferred_element_type=F32)
    m = m + _sigmoid_tanh(ld(gc_ref)) * jnp.dot(o_c.astype(BF16), wbr_ref[2], preferred_element_type=F32)
    o_ref[...] = x_ref[...] + jnp.dot(m.astype(BF16), wout_ref[...], preferred_element_type=F32)


def _merge(x, p, oaf, oab, obf, obb, hcf, hcb, nwa, nwb, wbr, wout):
    n_tok = x.shape[0]
    tok = lambda width, offset: pl.BlockSpec((T_MERGE, width), lambda i: (i, offset // width))

    def whole(a):
        return pl.BlockSpec(a.shape, lambda i: (0,) * a.ndim)

    return pl.pallas_call(
        _merge_kernel,
        grid=(n_tok // T_MERGE,),
        in_specs=[tok(D_MODEL, 0), tok(WIDTH, 0), tok(WIDTH, 0), tok(WIDTH, COL_OGA),
                  tok(WIDTH, 0), tok(WIDTH, 0), tok(WIDTH, COL_OGB),
                  tok(WIDTH, 0), tok(WIDTH, 0), tok(WIDTH, COL_YC),
                  tok(D_MODEL, COL_GA), tok(D_MODEL, COL_GB), tok(D_MODEL, COL_GC),
                  whole(nwa), whole(nwb), whole(wbr), whole(wout)],
        out_specs=tok(D_MODEL, 0),
        out_shape=jax.ShapeDtypeStruct((n_tok, D_MODEL), F32),
        compiler_params=pltpu.CompilerParams(
            dimension_semantics=("arbitrary",), vmem_limit_bytes=VMEM_LIMIT),
        name="merge",
    )(x, oaf, oab, p, obf, obb, p, hcf, hcb, p, p, p, p, nwa, nwb, wbr, wout)


def _ffn_kernel(x_ref, xp_ref, xn_ref, nw_ref, wg_ref, wv_ref, cwg_ref, cwv_ref, cbg_ref, cbv_ref,
                wd_ref, fw_ref, o_ref, h_ref, acc_ref, *, final_norm):
    i = pl.program_id(0)
    j = pl.program_id(1)
    blocks_per_seq = SEQ // T_FFN

    @pl.when(j == 0)
    def _():
        nw = nw_ref[...]
        first = (i % blocks_per_seq) == 0
        last = (i % blocks_per_seq) == blocks_per_seq - 1
        h_ref[0:SUBLANES, :] = jnp.where(first, 0.0, _rms(xp_ref[...], nw)).astype(BF16)
        h_ref[SUBLANES:SUBLANES + T_FFN, :] = _rms(x_ref[...], nw).astype(BF16)
        h_ref[SUBLANES + T_FFN:, :] = jnp.where(last, 0.0, _rms(xn_ref[...], nw)).astype(BF16)
        acc_ref[...] = x_ref[...]

    def up(r):
        h = h_ref[r * T_FFN_SUB:(r + 1) * T_FFN_SUB + 2 * SUBLANES, :]
        return (jnp.dot(h, wg_ref[...], preferred_element_type=F32),
                jnp.dot(h, wv_ref[...], preferred_element_type=F32))

    def conv(u, cw_ref, cb_ref):
        out = cb_ref[...]
        for t in range(FFN_CONV):
            off = SUBLANES + t - FFN_CONV // 2
            out = out + u[off:off + T_FFN_SUB, :] * cw_ref[t:t + 1, :]
        return out

    n_sub = T_FFN // T_FFN_SUB
    pending = up(0)
    for r in range(n_sub):
        ug, uv = pending
        if r + 1 < n_sub:
            pending = up(r + 1)
        act = _gelu_tanh(conv(ug, cwg_ref, cbg_ref)) * conv(uv, cwv_ref, cbv_ref)
        rows = slice(r * T_FFN_SUB, (r + 1) * T_FFN_SUB)
        acc_ref[rows, :] += jnp.dot(act.astype(BF16), wd_ref[...], preferred_element_type=F32)

    @pl.when(j == pl.num_programs(1) - 1)
    def _():
        y = acc_ref[...]
        o_ref[...] = _rms(y, fw_ref[...]) if final_norm else y


def _ffn(x, nw, w_up, cw, cb, w_down, fw, final_norm):
    n_tok = x.shape[0]
    halo = T_FFN // SUBLANES
    last_halo = n_tok // SUBLANES - 1
    n_f = D_FF // F_TILE
    return pl.pallas_call(
        functools.partial(_ffn_kernel, final_norm=final_norm),
        grid=(n_tok // T_FFN, n_f),
        in_specs=[
            pl.BlockSpec((T_FFN, D_MODEL), lambda i, j: (i, 0)),
            pl.BlockSpec((SUBLANES, D_MODEL), lambda i, j: (jnp.maximum(i * halo - 1, 0), 0)),
            pl.BlockSpec((SUBLANES, D_MODEL), lambda i, j: (jnp.minimum((i + 1) * halo, last_halo), 0)),
            pl.BlockSpec((1, D_MODEL), lambda i, j: (0, 0)),
            pl.BlockSpec((D_MODEL, F_TILE), lambda i, j: (0, j)),
            pl.BlockSpec((D_MODEL, F_TILE), lambda i, j: (0, n_f + j)),
            pl.BlockSpec((FFN_CONV, F_TILE), lambda i, j: (0, j)),
            pl.BlockSpec((FFN_CONV, F_TILE), lambda i, j: (0, n_f + j)),
            pl.BlockSpec((1, F_TILE), lambda i, j: (0, j)),
            pl.BlockSpec((1, F_TILE), lambda i, j: (0, n_f + j)),
            pl.BlockSpec((F_TILE, D_MODEL), lambda i, j: (j, 0)),
            pl.BlockSpec((1, D_MODEL), lambda i, j: (0, 0)),
        ],
        out_specs=pl.BlockSpec((T_FFN, D_MODEL), lambda i, j: (i, 0)),
        out_shape=jax.ShapeDtypeStruct((n_tok, D_MODEL), F32),
        scratch_shapes=[pltpu.VMEM((T_FFN + 2 * SUBLANES, D_MODEL), BF16),
                        pltpu.VMEM((T_FFN, D_MODEL), F32)],
        compiler_params=pltpu.CompilerParams(
            dimension_semantics=("arbitrary", "arbitrary"), vmem_limit_bytes=VMEM_LIMIT),
        name="ffn",
    )(x, x, x, nw, w_up, w_up, cw, cw, cb, cb, w_down, fw)


def _layout_w_in(w_in_l):
    (q_a, zf_a, zb_a, i_a, og_a, q_b, k_b, v_b, og_b, lrf, lrb, x_c, y_c, g_a, g_b, g_c) = jnp.split(
        w_in_l, [512, 1024, 1536, 2048, 2560, 2816, 3072, 3584, 4096, 4112, 4128, 4640, 5152, 6176, 7200],
        axis=1)
    pad = jnp.zeros((D_MODEL, LANES - 2 * GLA_RANK), w_in_l.dtype)
    return jnp.concatenate([q_a, zf_a, zb_a, i_a, og_a, q_b, k_b, v_b, og_b, x_c, y_c, g_a, g_b, g_c,
                            lrf, lrb, pad], axis=1).astype(BF16)


def _layout_gla_up(up_w_l):
    out = jnp.zeros((2, LANES, B_KEY), F32)
    out = out.at[0, 0:GLA_RANK].set(up_w_l[0])
    out = out.at[1, GLA_RANK:2 * GLA_RANK].set(up_w_l[1])
    return out.astype(BF16)


def _layout_rglru_gates(wa_l, wx_l):
    def tiles(w):
        w = w.reshape(2, 2, 4, C_BLOCK, C_BLOCK)
        eye = jnp.eye(4, dtype=w.dtype)
        t = jnp.einsum('dhbkj,bc->dhbkcj', w, eye)
        return t.reshape(2, 2, 4 * C_BLOCK, 4 * C_BLOCK)
    return jnp.stack([tiles(wa_l), tiles(wx_l)], axis=1).astype(BF16)


def kernel(x_prompt, x_sample, norm_mix_w, w_in, hgrn_lb_logits, hgrn_norm_w, gla_up_w, gla_up_b,
           gla_norm_w, c_conv_w, c_conv_b, rglru_wa, rglru_ba, rglru_wx, rglru_bx, rglru_lam,
           w_branch, w_out, norm_ffn_w, ffn_up, ffn_conv_w, ffn_conv_b, ffn_down, final_norm_w):
    lb_all = jnp.cumsum(jax.nn.softmax(hgrn_lb_logits.astype(F32), axis=0), axis=0)
    lb_all = lb_all - lb_all[0]
    fw = final_norm_w.reshape(1, D_MODEL)

    xs = [x_prompt.reshape(-1, D_MODEL), x_sample.reshape(-1, D_MODEL)]
    for l in range(DEPTH):
        w_in_l = _layout_w_in(w_in[l])
        up_w_l = _layout_gla_up(gla_up_w[l])
        gate_w_l = _layout_rglru_gates(rglru_wa[l], rglru_wx[l])
        gate_b_l = jnp.stack([rglru_ba[l], rglru_bx[l]], axis=1)
        w_branch_l, w_out_l = w_branch[l].astype(BF16), w_out[l].astype(BF16)
        ffn_up_l, ffn_down_l = ffn_up[l].astype(BF16), ffn_down[l].astype(BF16)
        for n, x in enumerate(xs):
            p = _proj(x, norm_mix_w[l].reshape(1, D_MODEL), w_in_l)
            oaf, oab = _bidir_gla(p, "hgrn", (lb_all[l],))
            obf, obb = _bidir_gla(p, "gla", (up_w_l, gla_up_b[l].reshape(2, 1, B_KEY)))
            hcf, hcb = _rglru(p, c_conv_w[l], c_conv_b[l].reshape(1, WIDTH), gate_w_l, gate_b_l,
                              rglru_lam[l])
            x = _merge(x, p, oaf, oab, obf, obb, hcf, hcb,
                       hgrn_norm_w[l].reshape(1, HEAD_V), gla_norm_w[l].reshape(1, HEAD_V),
                       w_branch_l, w_out_l)
            xs[n] = _ffn(x, norm_ffn_w[l].reshape(1, D_MODEL), ffn_up_l, ffn_conv_w[l],
                         ffn_conv_b[l].reshape(1, 2 * D_FF), ffn_down_l, fw,
                         final_norm=(l == DEPTH - 1))

    return (xs[0].reshape(x_prompt.shape), xs[1].reshape(x_sample.shape))
```

```python
import functools
import math

import jax
import jax.numpy as jnp
from jax import lax
from jax.experimental import pallas as pl
from jax.experimental.pallas import tpu as pltpu

F32 = jnp.float32
BF16 = jnp.bfloat16
P_DTYPE = BF16
O_DTYPE = BF16

D_MODEL = 1024
SEQ = 8192
DEPTH = 4
WIDTH = 512
HEAD_V = 128
A_KEY = 512
B_KEY = 256
GLA_RANK = 16
GLA_NORMALIZER = 16.0
C_BLOCKS = 8
C_BLOCK = 64
C_CONV = 4
RG_C = 8.0
D_FF = 2816
FFN_CONV = 3
EPS = 1e-6

LANES = 128
SUBLANES = 8
HALO = 16
VMEM_LIMIT = 56 * 1024 * 1024

N_PROJ = 8320
COL_QA, COL_ZF, COL_ZB, COL_IA, COL_OGA = 0, 512, 1024, 1536, 2048
COL_QB, COL_KB, COL_VB, COL_OGB = 2560, 2816, 3072, 3584
COL_XC, COL_YC = 4096, 4608
COL_GA, COL_GB, COL_GC = 5120, 6144, 7168
COL_LR = 8192

SUB = 16
CHUNK = 64
N_SUB = CHUNK // SUB
PHASE_LAG = 4
T_MIX = 512
T_PROJ = 512
N_PROJ_TILE = 2048
T_MERGE = 512
T_FFN = 512
FFN_GROUPS = T_FFN // SUBLANES
SLAB_PITCH = FFN_GROUPS + SUBLANES
F_TILE = 256


def _sigmoid(x):
    return 1.0 / (1.0 + jnp.exp(-x))


def _sigmoid_tanh(x):
    return 0.5 + 0.5 * jnp.tanh(0.5 * x)


def _gelu_tanh(x):
    return 0.5 * x * (1.0 + jnp.tanh(math.sqrt(2.0 / math.pi) * (x + 0.044715 * (x * x * x))))


def _rms(x, w):
    ms = jnp.mean(x * x, axis=-1, keepdims=True)
    return x * lax.rsqrt(ms + EPS) * w


def _proj_kernel(x_ref, nw_ref, w_ref, o_ref):
    h = _rms(x_ref[...], nw_ref[...]).astype(BF16)
    for lo in range(0, N_PROJ, N_PROJ_TILE):
        hi = min(lo + N_PROJ_TILE, N_PROJ)
        o_ref[:, lo:hi] = jnp.dot(h, w_ref[:, lo:hi], preferred_element_type=F32).astype(o_ref.dtype)


def _proj(x, nw, w):
    n_tok = x.shape[0]
    return pl.pallas_call(
        _proj_kernel,
        grid=(n_tok // T_PROJ,),
        in_specs=[
            pl.BlockSpec((T_PROJ, D_MODEL), lambda i: (i, 0)),
            pl.BlockSpec((1, D_MODEL), lambda i: (0, 0), pipeline_mode=pl.Buffered(1)),
            pl.BlockSpec((D_MODEL, N_PROJ), lambda i: (0, 0), pipeline_mode=pl.Buffered(1)),
        ],
        out_specs=pl.BlockSpec((T_PROJ, N_PROJ), lambda i: (i, 0)),
        out_shape=jax.ShapeDtypeStruct((n_tok, N_PROJ), P_DTYPE),
        compiler_params=pltpu.CompilerParams(
            dimension_semantics=("arbitrary",), vmem_limit_bytes=VMEM_LIMIT),
        name="proj",
    )(x, nw, w)


def _tri_consts(reverse):
    i = lax.broadcasted_iota(jnp.int32, (CHUNK, CHUNK), 0)
    j = lax.broadcasted_iota(jnp.int32, (CHUNK, CHUNK), 1)
    causal = (j >= i) if reverse else (j <= i)
    same_sub = (i // SUB) == (j // SUB)
    tri = jnp.where(causal, 1.0, 0.0).astype(BF16)
    tri_sub = jnp.where(causal & same_sub, 1.0, 0.0).astype(BF16)
    return jnp.concatenate([tri, tri_sub], axis=0), causal


def _chunk_cumsum(lg, cum_mat):
    lg_hi = lg.astype(BF16)
    lg_lo = (lg - lg_hi.astype(F32)).astype(BF16)
    return jnp.dot(cum_mat, jnp.concatenate([lg_hi, lg_lo], axis=1), preferred_element_type=F32)


def _chunk_scores(cum, q, k, v_list, st_list, head_masks, reverse):
    n_heads = len(v_list)
    b = cum[:CHUNK, :LANES] + cum[:CHUNK, LANES:]
    b_loc = cum[CHUNK:, :LANES] + cum[CHUNK:, LANES:]
    r = b - b_loc
    q_loc = q * jnp.exp(b_loc)
    k_inv = k * jnp.exp(-b_loc)
    q_dec = (q_loc * jnp.exp(r)).astype(BF16)
    last = 0 if reverse else CHUNK - 1
    b_last = b[last:last + 1, :]
    k_tail = (k * jnp.exp(b_last - b)).astype(BF16)
    e_last = jnp.exp(b_last)

    q_loc_bf = q_loc.astype(BF16)
    k_groups = []
    q_groups = []
    for s in range(N_SUB):
        lo, hi = (s * SUB, CHUNK) if reverse else (0, (s + 1) * SUB)
        r_s = r[s * SUB:s * SUB + 1, :]
        kt = (k_inv[lo:hi] * jnp.exp(r_s - r[lo:hi])).astype(BF16)
        pieces = []
        if lo > 0:
            pieces.append(jnp.zeros((lo, LANES), BF16))
        pieces.append(kt)
        if hi < CHUNK:
            pieces.append(jnp.zeros((CHUNK - hi, LANES), BF16))
        k_groups.append(jnp.concatenate(pieces, axis=0) if len(pieces) > 1 else kt)
        qp = []
        if s > 0:
            qp.append(jnp.zeros((s * SUB, LANES), BF16))
        qp.append(q_loc_bf[s * SUB:(s + 1) * SUB])
        if s < N_SUB - 1:
            qp.append(jnp.zeros((CHUNK - (s + 1) * SUB, LANES), BF16))
        q_groups.append(jnp.concatenate(qp, axis=0))
    k_wide = jnp.concatenate(k_groups, axis=1)
    q_wide = jnp.concatenate(q_groups, axis=1)
    if head_masks[0] is None:
        q_stack = q_wide
    else:
        zero = jnp.zeros_like(q_wide)
        q_stack = jnp.concatenate([jnp.where(m[0], q_wide, zero) for m in head_masks], axis=0)
    att = lax.dot_general(q_stack, k_wide, (((1,), (1,)), ((), ())), preferred_element_type=F32)

    o_inter, upd = [], []
    for h in range(n_heads):
        qd = q_dec if head_masks[h] is None else jnp.where(head_masks[h][1], q_dec, jnp.zeros_like(q_dec))
        o_inter.append(lax.dot_general(qd, st_list[h].astype(BF16), (((1,), (1,)), ((), ())),
                                       preferred_element_type=F32))
        upd.append(lax.dot_general(v_list[h], k_tail, (((0,), (0,)), ((), ())),
                                   preferred_element_type=F32))
    return att, o_inter, upd, e_last


def _chunk_outputs(att, o_inter, upd, e_last, v_list, st_list, causal):
    outs, new_states = [], []
    for h in range(len(v_list)):
        att_h = jnp.where(causal, att[h * CHUNK:(h + 1) * CHUNK], 0.0).astype(BF16)
        outs.append(jnp.dot(att_h, v_list[h], preferred_element_type=F32) + o_inter[h])
        new_states.append(st_list[h] * e_last + upd[h])
    return outs, new_states


def _gla_kernel(*refs, mode, n_groups, heads_per_group):
    if mode == "hgrn":
        (qf_ref, zf_ref, vf_ref, qb_ref, zb_ref, vb_ref, lb_ref, of_ref, ob_ref, st_ref) = refs
    else:
        (qf_ref, kf_ref, vf_ref, lrf_ref, qb_ref, kb_ref, vb_ref, lrb_ref, upw_ref, upb_ref,
         of_ref, ob_ref, st_ref, lg_ref) = refs
    n_heads = n_groups * heads_per_group

    @pl.when(pl.program_id(1) == 0)
    def _():
        st_ref[...] = jnp.zeros_like(st_ref)

    if mode == "gla":
        for d, lr_ref in enumerate((lrf_ref, lrb_ref)):
            pre = jnp.dot(lr_ref[...].astype(BF16), upw_ref[d], preferred_element_type=F32) + upb_ref[d]
            log_sig = jnp.minimum(pre, 0.0) - jnp.log(1.0 + jnp.exp(-jnp.abs(pre)))
            lg_ref[d] = log_sig / GLA_NORMALIZER

    consts = [_tri_consts(False), _tri_consts(True)]
    if heads_per_group == 1:
        head_masks = [None]
    else:
        width = LANES // heads_per_group
        lane_w = lax.broadcasted_iota(jnp.int32, (CHUNK, N_SUB * LANES), 1) % LANES
        lane = lax.broadcasted_iota(jnp.int32, (CHUNK, LANES), 1)
        head_masks = [tuple((ln >= h * width) & (ln < (h + 1) * width) for ln in (lane_w, lane))
                      for h in range(heads_per_group)]
    q_scale = (LANES // heads_per_group) ** -0.5
    n_chunks = T_MIX // CHUNK

    chains = [(d, g) for g in range(n_groups) for d in range(2)]

    def body(c, carry):
        work = [dict() for _ in chains]

        def phase1(w, d, g):
            reverse = d == 1
            cc = (n_chunks - 1 - c) if reverse else c
            w["rows"] = rows = pl.ds(pl.multiple_of(cc * CHUNK, CHUNK), CHUNK)
            cols = slice(g * LANES, (g + 1) * LANES)
            q = (qb_ref if reverse else qf_ref)[rows, cols].astype(F32)
            if mode == "hgrn":
                z = (zb_ref if reverse else zf_ref)[rows, cols].astype(F32)
                lb = lb_ref[d:d + 1, cols]
                y = lb + (1.0 - lb) * _sigmoid(z)
                lg = jnp.log(y)
                w["k"] = 1.0 - y
                w["q"] = q * _sigmoid_tanh(q) * q_scale
            else:
                w["k"] = (kb_ref if reverse else kf_ref)[rows, cols].astype(F32)
                lg = lg_ref[d, rows, cols]
                w["q"] = q * q_scale
            w["cum"] = _chunk_cumsum(lg, consts[d][0])

        def phase2(w, d, g):
            v_ref = vb_ref if d == 1 else vf_ref
            w["hs"] = hs = [g * heads_per_group + h for h in range(heads_per_group)]
            w["v"] = [v_ref[w["rows"], h * HEAD_V:(h + 1) * HEAD_V].astype(BF16) for h in hs]
            w["st"] = [st_ref[d * n_heads + h] for h in hs]
            w["scores"] = _chunk_scores(w["cum"], w["q"], w["k"], w["v"], w["st"], head_masks, d == 1)

        def phase3(w, d, g):
            o_ref = ob_ref if d == 1 else of_ref
            outs, new_states = _chunk_outputs(*w["scores"], w["v"], w["st"], consts[d][1])
            for h, o, st in zip(w["hs"], outs, new_states):
                o_ref[w["rows"], h * HEAD_V:(h + 1) * HEAD_V] = o.astype(o_ref.dtype)
                st_ref[d * n_heads + h] = st

        phases = (phase1, phase2, phase3)
        for slot in range(len(chains) + PHASE_LAG * (len(phases) - 1)):
            for p, phase in enumerate(phases):
                n = slot - p * PHASE_LAG
                if 0 <= n < len(chains):
                    phase(work[n], *chains[n])
        return carry

    lax.fori_loop(0, n_chunks, body, 0, unroll=2)


def _bidir_gla(p, mode, extra):
    n_tok = p.shape[0]
    n_seq = n_tok // SEQ
    nb = SEQ // T_MIX
    fwd = lambda s, i: s * nb + i
    bwd = lambda s, i: s * nb + (nb - 1 - i)

    def col(width, offset, blk):
        return pl.BlockSpec((T_MIX, width), lambda s, i: (blk(s, i), offset // width))

    def whole(a):
        return pl.BlockSpec(a.shape, lambda s, i: (0,) * a.ndim)

    if mode == "hgrn":
        (lb,) = extra
        in_specs = [col(WIDTH, COL_QA, fwd), col(WIDTH, COL_ZF, fwd), col(WIDTH, COL_IA, fwd),
                    col(WIDTH, COL_QA, bwd), col(WIDTH, COL_ZB, bwd), col(WIDTH, COL_IA, bwd),
                    whole(lb)]
        args = [p] * 6 + [lb]
        n_groups, heads_per_group = 4, 1
        scratch = [pltpu.VMEM((8, HEAD_V, LANES), F32)]
    else:
        upw, upb = extra
        in_specs = [col(B_KEY, COL_QB, fwd), col(B_KEY, COL_KB, fwd), col(WIDTH, COL_VB, fwd),
                    col(LANES, COL_LR, fwd),
                    col(B_KEY, COL_QB, bwd), col(B_KEY, COL_KB, bwd), col(WIDTH, COL_VB, bwd),
                    col(LANES, COL_LR, bwd), whole(upw), whole(upb)]
        args = [p] * 8 + [upw, upb]
        n_groups, heads_per_group = 2, 2
        scratch = [pltpu.VMEM((8, HEAD_V, LANES), F32), pltpu.VMEM((2, T_MIX, B_KEY), F32)]
    out_spec_f = pl.BlockSpec((T_MIX, WIDTH), lambda s, i: (fwd(s, i), 0))
    out_spec_b = pl.BlockSpec((T_MIX, WIDTH), lambda s, i: (bwd(s, i), 0))
    return pl.pallas_call(
        functools.partial(_gla_kernel, mode=mode, n_groups=n_groups, heads_per_group=heads_per_group),
        grid=(n_seq, nb),
        in_specs=in_specs,
        out_specs=[out_spec_f, out_spec_b],
        out_shape=[jax.ShapeDtypeStruct((n_tok, WIDTH), O_DTYPE)] * 2,
        scratch_shapes=scratch,
        compiler_params=pltpu.CompilerParams(
            dimension_semantics=("arbitrary", "arbitrary"), vmem_limit_bytes=VMEM_LIMIT),
        name="mix_" + mode,
    )(*args)


def _rglru_kernel(xf_ref, xfp_ref, xfn_ref, xb_ref, xbp_ref, xbn_ref, cw_ref, cb_ref, gw_ref, gb_ref,
                  lam_ref, of_ref, ob_ref, ext_ref, a_ref, u_ref, carry_ref):
    i = pl.program_id(1)
    nb = pl.num_programs(1)

    @pl.when(i == 0)
    def _():
        carry_ref[...] = jnp.zeros_like(carry_ref)

    row = lax.broadcasted_iota(jnp.int32, (T_MIX, WIDTH), 0) % SUBLANES
    n_rows = T_MIX // SUBLANES

    for d in range(2):
        reverse = d == 1
        x_ref, xp_ref, xn_ref, o_ref = (xb_ref, xbp_ref, xbn_ref, ob_ref) if reverse else (
            xf_ref, xfp_ref, xfn_ref, of_ref)
        blk = (nb - 1 - i) if reverse else i
        ext_ref[0:SUBLANES, :] = jnp.where(blk == 0, 0.0, xp_ref[...].astype(F32)[HALO - SUBLANES:, :])
        ext_ref[SUBLANES:SUBLANES + T_MIX, :] = x_ref[...].astype(F32)
        ext_ref[SUBLANES + T_MIX:, :] = jnp.where(blk == nb - 1, 0.0, xn_ref[...].astype(F32)[:SUBLANES, :])
        xc = cb_ref[...]
        for j in range(C_CONV):
            off = SUBLANES + j - C_CONV // 2
            xc = xc + ext_ref[off:off + T_MIX, :] * cw_ref[j:j + 1, :]
        xc_bf = xc.astype(BF16)
        gates = []
        for g in range(2):
            halves = [jnp.dot(xc_bf[:, hh * 256:(hh + 1) * 256], gw_ref[d, g, hh],
                              preferred_element_type=F32) for hh in range(2)]
            gates.append(_sigmoid_tanh(jnp.concatenate(halves, axis=1) + gb_ref[d, g:g + 1, :]))
        r_gate, i_gate = gates
        lam = lam_ref[d:d + 1, :]
        softplus_neg_lam = jnp.maximum(-lam, 0.0) + jnp.log(1.0 + jnp.exp(-jnp.abs(lam)))
        log_a = -RG_C * r_gate * softplus_neg_lam
        a = jnp.exp(log_a)
        u = jnp.sqrt(-jnp.tanh(log_a) * (1.0 + a * a)) * (i_gate * xc)
        for step in (1, 2, 4):
            shift = (T_MIX - step) if reverse else step
            a_sh = pltpu.roll(a, shift, 0)
            u_sh = pltpu.roll(u, shift, 0)
            keep = (row <= SUBLANES - 1 - step) if reverse else (row >= step)
            u = jnp.where(keep, a * u_sh + u, u)
            a = jnp.where(keep, a * a_sh, a)
        a_ref[...] = a
        u_ref[...] = u

        def body(n, h_prev):
            g = (n_rows - 1 - n) if reverse else n
            rows = pl.ds(pl.multiple_of(g * SUBLANES, SUBLANES), SUBLANES)
            h = a_ref[rows, :] * h_prev + u_ref[rows, :]
            u_ref[rows, :] = h
            edge = h[0:1, :] if reverse else h[SUBLANES - 1:SUBLANES, :]
            return jnp.broadcast_to(edge, (SUBLANES, WIDTH))

        carry_ref[d] = lax.fori_loop(0, n_rows, body, carry_ref[d], unroll=4)
        o_ref[...] = u_ref[...].astype(o_ref.dtype)


def _rglru(p, cw, cb, gw, gb, lam):
    n_tok = p.shape[0]
    n_seq = n_tok // SEQ
    nb = SEQ // T_MIX
    halo = T_MIX // HALO
    last_halo = n_tok // HALO - 1
    xcol = COL_XC // WIDTH
    fwd = lambda s, i: s * nb + i
    bwd = lambda s, i: s * nb + (nb - 1 - i)

    def specs(blk):
        return [
            pl.BlockSpec((T_MIX, WIDTH), lambda s, i: (blk(s, i), xcol)),
            pl.BlockSpec((HALO, WIDTH), lambda s, i: (jnp.maximum(blk(s, i) * halo - 1, 0), xcol)),
            pl.BlockSpec((HALO, WIDTH),
                         lambda s, i: (jnp.minimum((blk(s, i) + 1) * halo, last_halo), xcol)),
        ]

    def whole(a):
        return pl.BlockSpec(a.shape, lambda s, i: (0,) * a.ndim)

    return pl.pallas_call(
        _rglru_kernel,
        grid=(n_seq, nb),
        in_specs=specs(fwd) + specs(bwd) + [whole(cw), whole(cb), whole(gw), whole(gb), whole(lam)],
        out_specs=[pl.BlockSpec((T_MIX, WIDTH), lambda s, i: (fwd(s, i), 0)),
                   pl.BlockSpec((T_MIX, WIDTH), lambda s, i: (bwd(s, i), 0))],
        out_shape=[jax.ShapeDtypeStruct((n_tok, WIDTH), O_DTYPE)] * 2,
        scratch_shapes=[pltpu.VMEM((T_MIX + 2 * SUBLANES, WIDTH), F32),
                        pltpu.VMEM((T_MIX, WIDTH), F32),
                        pltpu.VMEM((T_MIX, WIDTH), F32),
                        pltpu.VMEM((2, SUBLANES, WIDTH), F32)],
        compiler_params=pltpu.CompilerParams(
            dimension_semantics=("arbitrary", "arbitrary"), vmem_limit_bytes=VMEM_LIMIT),
        name="mix_rglru",
    )(p, p, p, p, p, p, cw, cb, gw, gb, lam)


def _head_norm(o, w):
    parts = [_rms(o[:, h * HEAD_V:(h + 1) * HEAD_V], w) for h in range(WIDTH // HEAD_V)]
    return jnp.concatenate(parts, axis=1)


def _merge_kernel(x_ref, oaf_ref, oab_ref, oga_ref, obf_ref, obb_ref, ogb_ref, hcf_ref, hcb_ref, yc_ref,
                  ga_ref, gb_ref, gc_ref, nwa_ref, nwb_ref, wbr_ref, wout_ref, o_ref):
    ld = lambda ref: ref[...].astype(F32)
    oga = ld(oga_ref)
    ogb = ld(ogb_ref)
    o_a = _head_norm(ld(oaf_ref) + ld(oab_ref), nwa_ref[...]) * (oga * _sigmoid_tanh(oga))
    o_b = _head_norm(ld(obf_ref) + ld(obb_ref), nwb_ref[...]) * (ogb * _sigmoid_tanh(ogb))
    o_c = (ld(hcf_ref) + ld(hcb_ref)) * _gelu_tanh(ld(yc_ref))
    m = _sigmoid_tanh(ld(ga_ref)) * jnp.dot(o_a.astype(BF16), wbr_ref[0], preferred_element_type=F32)
    m = m + _sigmoid_tanh(ld(gb_ref)) * jnp.dot(o_b.astype(BF16), wbr_ref[1], preferred_element_type=F32)
    m = m + _sigmoid_tanh(ld(gc_ref)) * jnp.dot(o_c.astype(BF16), wbr_ref[2], preferred_element_type=F32)
    o_ref[...] = x_ref[...] + jnp.dot(m.astype(BF16), wout_ref[...], preferred_element_type=F32)


def _merge(x, p, oaf, oab, obf, obb, hcf, hcb, nwa, nwb, wbr, wout):
    n_tok = x.shape[0]
    tok = lambda width, offset: pl.BlockSpec((T_MERGE, width), lambda i: (i, offset // width))

    def whole(a):
        return pl.BlockSpec(a.shape, lambda i: (0,) * a.ndim)

    return pl.pallas_call(
        _merge_kernel,
        grid=(n_tok // T_MERGE,),
        in_specs=[tok(D_MODEL, 0), tok(WIDTH, 0), tok(WIDTH, 0), tok(WIDTH, COL_OGA),
                  tok(WIDTH, 0), tok(WIDTH, 0), tok(WIDTH, COL_OGB),
                  tok(WIDTH, 0), tok(WIDTH, 0), tok(WIDTH, COL_YC),
                  tok(D_MODEL, COL_GA), tok(D_MODEL, COL_GB), tok(D_MODEL, COL_GC),
                  whole(nwa), whole(nwb), whole(wbr), whole(wout)],
        out_specs=tok(D_MODEL, 0),
        out_shape=jax.ShapeDtypeStruct((n_tok, D_MODEL), F32),
        compiler_params=pltpu.CompilerParams(
            dimension_semantics=("arbitrary",), vmem_limit_bytes=VMEM_LIMIT),
        name="merge",
    )(x, oaf, oab, p, obf, obb, p, hcf, hcb, p, p, p, p, nwa, nwb, wbr, wout)


def _ffn_kernel(x_ref, xp_ref, xn_ref, nw_ref, wup_ref, cw_ref, cb_ref, wd_ref, fw_ref, o_ref, h_ref,
                slab_ref, *, final_norm):
    i = pl.program_id(0)
    blocks_per_seq = SEQ // T_FFN
    first = (i % blocks_per_seq) == 0
    last = (i % blocks_per_seq) == blocks_per_seq - 1
    nw = nw_ref[...]

    n_slabs = D_MODEL // LANES

    for c in range(n_slabs):
        for s in range(SUBLANES):
            slab_ref[c, s * SLAB_PITCH:s * SLAB_PITCH + FFN_GROUPS, :] = x_ref[
                s * FFN_GROUPS:(s + 1) * FFN_GROUPS, c * LANES:(c + 1) * LANES]

    def gather(start, stride):
        return jnp.concatenate([slab_ref[c, pl.ds(start, SUBLANES, stride=stride), :]
                                for c in range(n_slabs)], axis=1)

    for jj in range(FFN_GROUPS // 2):
        xb = jnp.concatenate([gather(2 * jj, SLAB_PITCH), gather(2 * jj + 1, SLAB_PITCH)], axis=0)
        h_ref[2 * SUBLANES * jj:2 * SUBLANES * (jj + 1), :] = _rms(xb, nw).astype(BF16)
    halo = jnp.concatenate([jnp.where(first, 0.0, _rms(xp_ref[...], nw)),
                            jnp.where(last, 0.0, _rms(xn_ref[...], nw))], axis=0)
    h_ref[T_FFN:, :] = halo.astype(BF16)

    def up(f):
        h = h_ref[...]
        return tuple(jnp.dot(h, wup_ref[:, c * D_FF + f * F_TILE:c * D_FF + (f + 1) * F_TILE],
                             preferred_element_type=F32) for c in range(2))

    def conv(u, c, f):
        cols = slice(c * D_FF + f * F_TILE, c * D_FF + (f + 1) * F_TILE)
        body = u[:T_FFN]
        before = u[T_FFN + SUBLANES - 1:T_FFN + SUBLANES]
        after = u[T_FFN + SUBLANES:T_FFN + SUBLANES + 1]
        prev0 = jnp.concatenate([before, body[T_FFN - SUBLANES:T_FFN - 1]], axis=0)
        next_last = jnp.concatenate([body[1:SUBLANES], after], axis=0)
        u_prev = jnp.concatenate([prev0, body[:T_FFN - SUBLANES]], axis=0)
        u_next = jnp.concatenate([body[SUBLANES:], next_last], axis=0)
        return (cb_ref[:, cols] + u_prev * cw_ref[0:1, cols] + body * cw_ref[1:2, cols]
                + u_next * cw_ref[2:3, cols])

    n_f = D_FF // F_TILE
    acc = None
    pending = up(0)
    for f in range(n_f):
        ug, uv = pending
        if f + 1 < n_f:
            pending = up(f + 1)
        act = _gelu_tanh(conv(ug, 0, f)) * conv(uv, 1, f)
        down = jnp.dot(act.astype(BF16), wd_ref[f * F_TILE:(f + 1) * F_TILE, :],
                       preferred_element_type=F32)
        acc = down if acc is None else acc + down

    for c in range(n_slabs):
        slab_ref[c, 0:T_FFN, :] = acc[:, c * LANES:(c + 1) * LANES]
    groups_per_sublane = FFN_GROUPS // SUBLANES
    for k in range(FFN_GROUPS):
        rows = slice(k * SUBLANES, (k + 1) * SUBLANES)
        s, j0 = k // groups_per_sublane, (k % groups_per_sublane) * SUBLANES
        y = x_ref[rows, :] + gather(j0 * SUBLANES + s, SUBLANES)
        o_ref[rows, :] = _rms(y, fw_ref[...]) if final_norm else y


def _ffn(x, nw, w_up, cw, cb, w_down, fw, final_norm):
    n_tok = x.shape[0]
    halo = T_FFN // SUBLANES
    last_halo = n_tok // SUBLANES - 1

    def resident(a):
        return pl.BlockSpec(a.shape, lambda i: (0,) * a.ndim, pipeline_mode=pl.Buffered(1))

    return pl.pallas_call(
        functools.partial(_ffn_kernel, final_norm=final_norm),
        grid=(n_tok // T_FFN,),
        in_specs=[
            pl.BlockSpec((T_FFN, D_MODEL), lambda i: (i, 0)),
            pl.BlockSpec((SUBLANES, D_MODEL), lambda i: (jnp.maximum(i * halo - 1, 0), 0)),
            pl.BlockSpec((SUBLANES, D_MODEL), lambda i: (jnp.minimum((i + 1) * halo, last_halo), 0)),
            resident(nw), resident(w_up), resident(cw), resident(cb), resident(w_down), resident(fw),
        ],
        out_specs=pl.BlockSpec((T_FFN, D_MODEL), lambda i: (i, 0)),
        out_shape=jax.ShapeDtypeStruct((n_tok, D_MODEL), F32),
        scratch_shapes=[pltpu.VMEM((T_FFN + 2 * SUBLANES, D_MODEL), BF16),
                        pltpu.VMEM((D_MODEL // LANES, SUBLANES * SLAB_PITCH, LANES), F32)],
        compiler_params=pltpu.CompilerParams(
            dimension_semantics=("arbitrary",), vmem_limit_bytes=VMEM_LIMIT),
        name="ffn",
    )(x, x, x, nw, w_up, cw, cb, w_down, fw)


def _layout_w_in(w_in_l):
    (q_a, zf_a, zb_a, i_a, og_a, q_b, k_b, v_b, og_b, lrf, lrb, x_c, y_c, g_a, g_b, g_c) = jnp.split(
        w_in_l, [512, 1024, 1536, 2048, 2560, 2816, 3072, 3584, 4096, 4112, 4128, 4640, 5152, 6176, 7200],
        axis=1)
    pad = jnp.zeros((D_MODEL, LANES - 2 * GLA_RANK), w_in_l.dtype)
    return jnp.concatenate([q_a, zf_a, zb_a, i_a, og_a, q_b, k_b, v_b, og_b, x_c, y_c, g_a, g_b, g_c,
                            lrf, lrb, pad], axis=1).astype(BF16)


def _layout_gla_up(up_w_l):
    out = jnp.zeros((2, LANES, B_KEY), F32)
    out = out.at[0, 0:GLA_RANK].set(up_w_l[0])
    out = out.at[1, GLA_RANK:2 * GLA_RANK].set(up_w_l[1])
    return out.astype(BF16)


def _layout_rglru_gates(wa_l, wx_l):
    def tiles(w):
        w = w.reshape(2, 2, 4, C_BLOCK, C_BLOCK)
        eye = jnp.eye(4, dtype=w.dtype)
        t = jnp.einsum('dhbkj,bc->dhbkcj', w, eye)
        return t.reshape(2, 2, 4 * C_BLOCK, 4 * C_BLOCK)
    return jnp.stack([tiles(wa_l), tiles(wx_l)], axis=1).astype(BF16)


def kernel(x_prompt, x_sample, norm_mix_w, w_in, hgrn_lb_logits, hgrn_norm_w, gla_up_w, gla_up_b,
           gla_norm_w, c_conv_w, c_conv_b, rglru_wa, rglru_ba, rglru_wx, rglru_bx, rglru_lam,
           w_branch, w_out, norm_ffn_w, ffn_up, ffn_conv_w, ffn_conv_b, ffn_down, final_norm_w):
    lb_all = jnp.cumsum(jax.nn.softmax(hgrn_lb_logits.astype(F32), axis=0), axis=0)
    lb_all = lb_all - lb_all[0]
    fw = final_norm_w.reshape(1, D_MODEL)

    xs = [x_prompt.reshape(-1, D_MODEL), x_sample.reshape(-1, D_MODEL)]
    for l in range(DEPTH):
        w_in_l = _layout_w_in(w_in[l])
        up_w_l = _layout_gla_up(gla_up_w[l])
        gate_w_l = _layout_rglru_gates(rglru_wa[l], rglru_wx[l])
        gate_b_l = jnp.stack([rglru_ba[l], rglru_bx[l]], axis=1)
        w_branch_l, w_out_l = w_branch[l].astype(BF16), w_out[l].astype(BF16)
        ffn_up_l, ffn_down_l = ffn_up[l].astype(BF16), ffn_down[l].astype(BF16)
        for n, x in enumerate(xs):
            p = _proj(x, norm_mix_w[l].reshape(1, D_MODEL), w_in_l)
            oaf, oab = _bidir_gla(p, "hgrn", (lb_all[l],))
            obf, obb = _bidir_gla(p, "gla", (up_w_l, gla_up_b[l].reshape(2, 1, B_KEY)))
            hcf, hcb = _rglru(p, c_conv_w[l], c_conv_b[l].reshape(1, WIDTH), gate_w_l, gate_b_l,
                              rglru_lam[l])
            x = _merge(x, p, oaf, oab, obf, obb, hcf, hcb,
                       hgrn_norm_w[l].reshape(1, HEAD_V), gla_norm_w[l].reshape(1, HEAD_V),
                       w_branch_l, w_out_l)
            xs[n] = _ffn(x, norm_ffn_w[l].reshape(1, D_MODEL), ffn_up_l, ffn_conv_w[l],
                         ffn_conv_b[l].reshape(1, 2 * D_FF), ffn_down_l, fw,
                         final_norm=(l == DEPTH - 1))

    return (xs[0].reshape(x_prompt.shape), xs[1].reshape(x_sample.shape))
```

```python
import functools
import math

import jax
import jax.numpy as jnp
from jax import lax
from jax.experimental import pallas as pl
from jax.experimental.pallas import tpu as pltpu

F32 = jnp.float32
BF16 = jnp.bfloat16
P_DTYPE = BF16
O_DTYPE = BF16

D_MODEL = 1024
SEQ = 8192
DEPTH = 4
WIDTH = 512
HEAD_V = 128
A_KEY = 512
B_KEY = 256
GLA_RANK = 16
GLA_NORMALIZER = 16.0
GLA_Q_SCALE = 0.125
C_BLOCKS = 8
C_BLOCK = 64
C_CONV = 4
RG_C = 8.0
D_FF = 2816
FFN_CONV = 3
EPS = 1e-6

LANES = 128
SUBLANES = 8
HALO = 16
VMEM_LIMIT = 56 * 1024 * 1024

N_PROJ = 8320
COL_QA, COL_ZF, COL_ZB, COL_IA, COL_OGA = 0, 512, 1024, 1536, 2048
COL_QB, COL_KB, COL_VB, COL_OGB = 2560, 2816, 3072, 3584
COL_XC, COL_YC = 4096, 4608
COL_GA, COL_GB, COL_GC = 5120, 6144, 7168
COL_LR = 8192

SUB = 16
CHUNK = 64
N_SUB = CHUNK // SUB
PHASE_LAG = 4
T_MIX = 1024
MIX_GROUPS = T_MIX // SUBLANES
MIX_PITCH = MIX_GROUPS + SUBLANES
T_PROJ = 512
N_PROJ_TILE = 2048
T_MERGE = 512
T_FFN = 512
FFN_GROUPS = T_FFN // SUBLANES
SLAB_PITCH = FFN_GROUPS + SUBLANES
F_TILE = 256


def _sigmoid(x):
    return 1.0 / (1.0 + jnp.exp(-x))


def _sigmoid_tanh(x):
    return 0.5 + 0.5 * jnp.tanh(0.5 * x)


def _gelu_tanh(x):
    return 0.5 * x * (1.0 + jnp.tanh(math.sqrt(2.0 / math.pi) * (x + 0.044715 * (x * x * x))))


def _rms(x, w):
    ms = jnp.mean(x * x, axis=-1, keepdims=True)
    return x * lax.rsqrt(ms + EPS) * w


def _proj_activation(col, u, lb_ref):
    if col == COL_QA:
        return u * _sigmoid_tanh(u) * (HEAD_V ** -0.5)
    if col in (COL_ZF, COL_ZB):
        lb = lb_ref[(col - COL_ZF) // WIDTH:(col - COL_ZF) // WIDTH + 1, :]
        return jnp.log(lb + (1.0 - lb) * _sigmoid(u))
    if col in (COL_OGA, COL_OGB):
        return u * _sigmoid_tanh(u)
    if col == COL_YC:
        return _gelu_tanh(u)
    if COL_GA <= col < COL_LR:
        return _sigmoid_tanh(u)
    return u


def _proj_kernel(x_ref, nw_ref, lb_ref, w_ref, o_ref):
    h = _rms(x_ref[...], nw_ref[...]).astype(BF16)
    for lo in range(0, N_PROJ, N_PROJ_TILE):
        hi = min(lo + N_PROJ_TILE, N_PROJ)
        u = jnp.dot(h, w_ref[:, lo:hi], preferred_element_type=F32)
        for col in range(lo, hi, WIDTH):
            end = min(col + WIDTH, hi)
            o_ref[:, col:end] = _proj_activation(col, u[:, col - lo:end - lo], lb_ref).astype(o_ref.dtype)


def _proj(x, nw, lb, w):
    n_tok = x.shape[0]
    return pl.pallas_call(
        _proj_kernel,
        grid=(n_tok // T_PROJ,),
        in_specs=[
            pl.BlockSpec((T_PROJ, D_MODEL), lambda i: (i, 0)),
            pl.BlockSpec((1, D_MODEL), lambda i: (0, 0), pipeline_mode=pl.Buffered(1)),
            pl.BlockSpec((2, WIDTH), lambda i: (0, 0), pipeline_mode=pl.Buffered(1)),
            pl.BlockSpec((D_MODEL, N_PROJ), lambda i: (0, 0), pipeline_mode=pl.Buffered(1)),
        ],
        out_specs=pl.BlockSpec((T_PROJ, N_PROJ), lambda i: (i, 0)),
        out_shape=jax.ShapeDtypeStruct((n_tok, N_PROJ), P_DTYPE),
        compiler_params=pltpu.CompilerParams(
            dimension_semantics=("arbitrary",), vmem_limit_bytes=VMEM_LIMIT),
        name="proj",
    )(x, nw, lb, w)


def _tri_consts(reverse):
    i = lax.broadcasted_iota(jnp.int32, (CHUNK, CHUNK), 0)
    j = lax.broadcasted_iota(jnp.int32, (CHUNK, CHUNK), 1)
    causal = (j >= i) if reverse else (j <= i)
    same_sub = (i // SUB) == (j // SUB)
    tri = jnp.where(causal, 1.0, 0.0).astype(BF16)
    tri_sub = jnp.where(causal & same_sub, 1.0, 0.0).astype(BF16)
    return jnp.concatenate([tri, tri_sub], axis=0), causal


def _chunk_cumsum(lg, cum_mat):
    lg_hi = lg.astype(BF16)
    lg_lo = (lg - lg_hi.astype(F32)).astype(BF16)
    return jnp.dot(cum_mat, jnp.concatenate([lg_hi, lg_lo], axis=1), preferred_element_type=F32)


def _chunk_scores(cum, q, k, v_list, st_list, head_masks, reverse):
    n_heads = len(v_list)
    b = cum[:CHUNK, :LANES] + cum[:CHUNK, LANES:]
    b_loc = cum[CHUNK:, :LANES] + cum[CHUNK:, LANES:]
    r = b - b_loc
    q_loc = q * jnp.exp(b_loc)
    k_inv = k * jnp.exp(-b_loc)
    q_dec = (q_loc * jnp.exp(r)).astype(BF16)
    last = 0 if reverse else CHUNK - 1
    b_last = b[last:last + 1, :]
    k_tail = (k * jnp.exp(b_last - b)).astype(BF16)
    e_last = jnp.exp(b_last)

    q_loc_bf = q_loc.astype(BF16)
    k_groups = []
    q_groups = []
    for s in range(N_SUB):
        lo, hi = (s * SUB, CHUNK) if reverse else (0, (s + 1) * SUB)
        r_s = r[s * SUB:s * SUB + 1, :]
        kt = (k_inv[lo:hi] * jnp.exp(r_s - r[lo:hi])).astype(BF16)
        pieces = []
        if lo > 0:
            pieces.append(jnp.zeros((lo, LANES), BF16))
        pieces.append(kt)
        if hi < CHUNK:
            pieces.append(jnp.zeros((CHUNK - hi, LANES), BF16))
        k_groups.append(jnp.concatenate(pieces, axis=0) if len(pieces) > 1 else kt)
        qp = []
        if s > 0:
            qp.append(jnp.zeros((s * SUB, LANES), BF16))
        qp.append(q_loc_bf[s * SUB:(s + 1) * SUB])
        if s < N_SUB - 1:
            qp.append(jnp.zeros((CHUNK - (s + 1) * SUB, LANES), BF16))
        q_groups.append(jnp.concatenate(qp, axis=0))
    k_wide = jnp.concatenate(k_groups, axis=1)
    q_wide = jnp.concatenate(q_groups, axis=1)
    if head_masks[0] is None:
        q_stack = q_wide
    else:
        zero = jnp.zeros_like(q_wide)
        q_stack = jnp.concatenate([jnp.where(m[0], q_wide, zero) for m in head_masks], axis=0)
    att = lax.dot_general(q_stack, k_wide, (((1,), (1,)), ((), ())), preferred_element_type=F32)

    o_inter, upd = [], []
    for h in range(n_heads):
        qd = q_dec if head_masks[h] is None else jnp.where(head_masks[h][1], q_dec, jnp.zeros_like(q_dec))
        o_inter.append(lax.dot_general(qd, st_list[h].astype(BF16), (((1,), (1,)), ((), ())),
                                       preferred_element_type=F32))
        upd.append(lax.dot_general(v_list[h], k_tail, (((0,), (0,)), ((), ())),
                                   preferred_element_type=F32))
    return att, o_inter, upd, e_last


def _chunk_outputs(att, o_inter, upd, e_last, v_list, st_list, causal):
    outs, new_states = [], []
    for h in range(len(v_list)):
        att_h = jnp.where(causal, att[h * CHUNK:(h + 1) * CHUNK], 0.0).astype(BF16)
        outs.append(jnp.dot(att_h, v_list[h], preferred_element_type=F32) + o_inter[h])
        new_states.append(st_list[h] * e_last + upd[h])
    return outs, new_states


def _gla_kernel(*refs, mode, n_groups, heads_per_group):
    if mode == "hgrn":
        (qf_ref, zf_ref, vf_ref, qb_ref, zb_ref, vb_ref, of_ref, ob_ref, st_ref) = refs
    else:
        (qf_ref, kf_ref, vf_ref, lrf_ref, qb_ref, kb_ref, vb_ref, lrb_ref, upw_ref, upb_ref,
         of_ref, ob_ref, st_ref, lg_ref) = refs
    n_heads = n_groups * heads_per_group

    @pl.when(pl.program_id(1) == 0)
    def _():
        st_ref[...] = jnp.zeros_like(st_ref)

    if mode == "gla":
        for d, lr_ref in enumerate((lrf_ref, lrb_ref)):
            pre = jnp.dot(lr_ref[...].astype(BF16), upw_ref[d], preferred_element_type=F32) + upb_ref[d]
            log_sig = jnp.minimum(pre, 0.0) - jnp.log(1.0 + jnp.exp(-jnp.abs(pre)))
            lg_ref[d] = log_sig / GLA_NORMALIZER

    consts = [_tri_consts(False), _tri_consts(True)]
    if heads_per_group == 1:
        head_masks = [None]
    else:
        width = LANES // heads_per_group
        lane_w = lax.broadcasted_iota(jnp.int32, (CHUNK, N_SUB * LANES), 1) % LANES
        lane = lax.broadcasted_iota(jnp.int32, (CHUNK, LANES), 1)
        head_masks = [tuple((ln >= h * width) & (ln < (h + 1) * width) for ln in (lane_w, lane))
                      for h in range(heads_per_group)]
    n_chunks = T_MIX // CHUNK

    chains = [(d, g) for g in range(n_groups) for d in range(2)]

    def body(c, carry):
        work = [dict() for _ in chains]

        def phase1(w, d, g):
            reverse = d == 1
            cc = (n_chunks - 1 - c) if reverse else c
            w["rows"] = rows = pl.ds(pl.multiple_of(cc * CHUNK, CHUNK), CHUNK)
            cols = slice(g * LANES, (g + 1) * LANES)
            w["q"] = (qb_ref if reverse else qf_ref)[rows, cols].astype(F32)
            if mode == "hgrn":
                lg = (zb_ref if reverse else zf_ref)[rows, cols].astype(F32)
                w["k"] = 1.0 - jnp.exp(lg)
            else:
                w["k"] = (kb_ref if reverse else kf_ref)[rows, cols].astype(F32)
                lg = lg_ref[d, rows, cols]
            w["cum"] = _chunk_cumsum(lg, consts[d][0])

        def phase2(w, d, g):
            v_ref = vb_ref if d == 1 else vf_ref
            w["hs"] = hs = [g * heads_per_group + h for h in range(heads_per_group)]
            w["v"] = [v_ref[w["rows"], h * HEAD_V:(h + 1) * HEAD_V].astype(BF16) for h in hs]
            w["st"] = [st_ref[d * n_heads + h] for h in hs]
            w["scores"] = _chunk_scores(w["cum"], w["q"], w["k"], w["v"], w["st"], head_masks, d == 1)

        def phase3(w, d, g):
            o_ref = ob_ref if d == 1 else of_ref
            outs, new_states = _chunk_outputs(*w["scores"], w["v"], w["st"], consts[d][1])
            for h, o, st in zip(w["hs"], outs, new_states):
                o_ref[w["rows"], h * HEAD_V:(h + 1) * HEAD_V] = o.astype(o_ref.dtype)
                st_ref[d * n_heads + h] = st

        phases = (phase1, phase2, phase3)
        for slot in range(len(chains) + PHASE_LAG * (len(phases) - 1)):
            for p, phase in enumerate(phases):
                n = slot - p * PHASE_LAG
                if 0 <= n < len(chains):
                    phase(work[n], *chains[n])
        return carry

    lax.fori_loop(0, n_chunks, body, 0, unroll=2)


def _bidir_gla(p, mode, extra):
    n_tok = p.shape[0]
    n_seq = n_tok // SEQ
    nb = SEQ // T_MIX
    fwd = lambda s, i: s * nb + i
    bwd = lambda s, i: s * nb + (nb - 1 - i)

    def col(width, offset, blk):
        return pl.BlockSpec((T_MIX, width), lambda s, i: (blk(s, i), offset // width))

    def whole(a):
        return pl.BlockSpec(a.shape, lambda s, i: (0,) * a.ndim)

    if mode == "hgrn":
        in_specs = [col(WIDTH, COL_QA, fwd), col(WIDTH, COL_ZF, fwd), col(WIDTH, COL_IA, fwd),
                    col(WIDTH, COL_QA, bwd), col(WIDTH, COL_ZB, bwd), col(WIDTH, COL_IA, bwd)]
        args = [p] * 6
        n_groups, heads_per_group = 4, 1
        scratch = [pltpu.VMEM((8, HEAD_V, LANES), F32)]
    else:
        upw, upb = extra
        in_specs = [col(B_KEY, COL_QB, fwd), col(B_KEY, COL_KB, fwd), col(WIDTH, COL_VB, fwd),
                    col(LANES, COL_LR, fwd),
                    col(B_KEY, COL_QB, bwd), col(B_KEY, COL_KB, bwd), col(WIDTH, COL_VB, bwd),
                    col(LANES, COL_LR, bwd), whole(upw), whole(upb)]
        args = [p] * 8 + [upw, upb]
        n_groups, heads_per_group = 2, 2
        scratch = [pltpu.VMEM((8, HEAD_V, LANES), F32), pltpu.VMEM((2, T_MIX, B_KEY), F32)]
    out_spec_f = pl.BlockSpec((T_MIX, WIDTH), lambda s, i: (fwd(s, i), 0))
    out_spec_b = pl.BlockSpec((T_MIX, WIDTH), lambda s, i: (bwd(s, i), 0))
    return pl.pallas_call(
        functools.partial(_gla_kernel, mode=mode, n_groups=n_groups, heads_per_group=heads_per_group),
        grid=(n_seq, nb),
        in_specs=in_specs,
        out_specs=[out_spec_f, out_spec_b],
        out_shape=[jax.ShapeDtypeStruct((n_tok, WIDTH), O_DTYPE)] * 2,
        scratch_shapes=scratch,
        compiler_params=pltpu.CompilerParams(
            dimension_semantics=("arbitrary", "arbitrary"), vmem_limit_bytes=VMEM_LIMIT),
        name="mix_" + mode,
    )(*args)


def _rglru_kernel(xf_ref, xfp_ref, xfn_ref, xb_ref, xbp_ref, xbn_ref, cw_ref, cb_ref, gw_ref, gb_ref,
                  lam_ref, of_ref, ob_ref, slab_ref, a_ref, u_ref, carry_ref):
    i = pl.program_id(1)
    nb = pl.num_programs(1)

    @pl.when(i == 0)
    def _():
        carry_ref[...] = jnp.zeros_like(carry_ref)

    n_slabs = WIDTH // LANES
    top = T_MIX - SUBLANES

    def gather(start, stride):
        return jnp.concatenate([slab_ref[c, pl.ds(start, SUBLANES, stride=stride), :]
                                for c in range(n_slabs)], axis=1)

    for d in range(2):
        reverse = d == 1
        x_ref, xp_ref, xn_ref, o_ref = (xb_ref, xbp_ref, xbn_ref, ob_ref) if reverse else (
            xf_ref, xfp_ref, xfn_ref, of_ref)
        blk = (nb - 1 - i) if reverse else i
        xf = x_ref[...].astype(F32)
        for c in range(n_slabs):
            for s in range(SUBLANES):
                slab_ref[c, s * MIX_PITCH:s * MIX_PITCH + MIX_GROUPS, :] = xf[
                    s * MIX_GROUPS:(s + 1) * MIX_GROUPS, c * LANES:(c + 1) * LANES]
        groups = [gather(j, MIX_PITCH) for j in range(MIX_GROUPS)]
        x = jnp.concatenate(groups, axis=0)
        before = jnp.where(blk == 0, 0.0, xp_ref[...].astype(F32))
        after = jnp.where(blk == nb - 1, 0.0, xn_ref[...].astype(F32))
        down1 = jnp.concatenate([before[HALO - 1:HALO], groups[-1][:SUBLANES - 1]], axis=0)
        down2 = jnp.concatenate([before[HALO - 2:HALO - 1], groups[-2][:SUBLANES - 1]], axis=0)
        up1 = jnp.concatenate([groups[0][1:], after[0:1]], axis=0)
        x_m1 = jnp.concatenate([down1, x[:top]], axis=0)
        x_m2 = jnp.concatenate([down2, down1, x[:top - SUBLANES]], axis=0)
        x_p1 = jnp.concatenate([x[SUBLANES:], up1], axis=0)
        xc = (cb_ref[...] + x_m2 * cw_ref[0:1, :] + x_m1 * cw_ref[1:2, :] + x * cw_ref[2:3, :]
              + x_p1 * cw_ref[3:4, :])
        xc_bf = xc.astype(BF16)
        gates = []
        for g in range(2):
            halves = [jnp.dot(xc_bf[:, hh * 256:(hh + 1) * 256], gw_ref[d, g, hh],
                              preferred_element_type=F32) for hh in range(2)]
            gates.append(_sigmoid_tanh(jnp.concatenate(halves, axis=1) + gb_ref[d, g:g + 1, :]))
        r_gate, i_gate = gates
        lam = lam_ref[d:d + 1, :]
        softplus_neg_lam = jnp.maximum(-lam, 0.0) + jnp.log(1.0 + jnp.exp(-jnp.abs(lam)))
        log_a = -RG_C * r_gate * softplus_neg_lam
        a = jnp.exp(log_a)
        u = jnp.sqrt(-jnp.tanh(log_a) * (1.0 + a * a)) * (i_gate * xc)
        a_ref[...] = a
        u_ref[...] = u

        def body(n, state):
            h_prev, p_prev = state
            j = (MIX_GROUPS - 1 - n) if reverse else n
            rows = pl.ds(pl.multiple_of(j * SUBLANES, SUBLANES), SUBLANES)
            a_j = a_ref[rows, :]
            h = a_j * h_prev + u_ref[rows, :]
            p = a_j * p_prev
            u_ref[rows, :] = h
            a_ref[rows, :] = p
            return h, p

        init = (jnp.zeros((SUBLANES, WIDTH), F32), jnp.ones((SUBLANES, WIDTH), F32))
        h_end, p_end = lax.fori_loop(0, MIX_GROUPS, body, init, unroll=8)
        carry = carry_ref[d][0:1, :]
        carries = [None] * SUBLANES
        for s in (range(SUBLANES - 1, -1, -1) if reverse else range(SUBLANES)):
            carries[s] = carry
            carry = h_end[s:s + 1, :] + p_end[s:s + 1, :] * carry
        carry_ref[d] = jnp.broadcast_to(carry, (SUBLANES, WIDTH))
        carry_in = jnp.concatenate(carries, axis=0)
        h_all = u_ref[...] + a_ref[...] * jnp.concatenate([carry_in] * MIX_GROUPS, axis=0)
        for c in range(n_slabs):
            slab_ref[c, 0:T_MIX, :] = h_all[:, c * LANES:(c + 1) * LANES]
        groups_per_sublane = MIX_GROUPS // SUBLANES
        for k2 in range(MIX_GROUPS // 2):
            pair = []
            for k in (2 * k2, 2 * k2 + 1):
                s, j0 = k // groups_per_sublane, (k % groups_per_sublane) * SUBLANES
                pair.append(gather(j0 * SUBLANES + s, SUBLANES))
            o_ref[2 * SUBLANES * k2:2 * SUBLANES * (k2 + 1), :] = jnp.concatenate(pair, axis=0).astype(
                o_ref.dtype)


def _rglru(p, cw, cb, gw, gb, lam):
    n_tok = p.shape[0]
    n_seq = n_tok // SEQ
    nb = SEQ // T_MIX
    halo = T_MIX // HALO
    last_halo = n_tok // HALO - 1
    xcol = COL_XC // WIDTH
    fwd = lambda s, i: s * nb + i
    bwd = lambda s, i: s * nb + (nb - 1 - i)

    def specs(blk):
        return [
            pl.BlockSpec((T_MIX, WIDTH), lambda s, i: (blk(s, i), xcol)),
            pl.BlockSpec((HALO, WIDTH), lambda s, i: (jnp.maximum(blk(s, i) * halo - 1, 0), xcol)),
            pl.BlockSpec((HALO, WIDTH),
                         lambda s, i: (jnp.minimum((blk(s, i) + 1) * halo, last_halo), xcol)),
        ]

    def whole(a):
        return pl.BlockSpec(a.shape, lambda s, i: (0,) * a.ndim)

    return pl.pallas_call(
        _rglru_kernel,
        grid=(n_seq, nb),
        in_specs=specs(fwd) + specs(bwd) + [whole(cw), whole(cb), whole(gw), whole(gb), whole(lam)],
        out_specs=[pl.BlockSpec((T_MIX, WIDTH), lambda s, i: (fwd(s, i), 0)),
                   pl.BlockSpec((T_MIX, WIDTH), lambda s, i: (bwd(s, i), 0))],
        out_shape=[jax.ShapeDtypeStruct((n_tok, WIDTH), O_DTYPE)] * 2,
        scratch_shapes=[pltpu.VMEM((WIDTH // LANES, SUBLANES * MIX_PITCH, LANES), F32),
                        pltpu.VMEM((T_MIX, WIDTH), F32),
                        pltpu.VMEM((T_MIX, WIDTH), F32),
                        pltpu.VMEM((2, SUBLANES, WIDTH), F32)],
        compiler_params=pltpu.CompilerParams(
            dimension_semantics=("arbitrary", "arbitrary"), vmem_limit_bytes=VMEM_LIMIT),
        name="mix_rglru",
    )(p, p, p, p, p, p, cw, cb, gw, gb, lam)


def _head_norm(o, w):
    parts = [_rms(o[:, h * HEAD_V:(h + 1) * HEAD_V], w) for h in range(WIDTH // HEAD_V)]
    return jnp.concatenate(parts, axis=1)


def _merge_kernel(x_ref, oaf_ref, oab_ref, oga_ref, obf_ref, obb_ref, ogb_ref, hcf_ref, hcb_ref, yc_ref,
                  ga_ref, gb_ref, gc_ref, nwa_ref, nwb_ref, wbr_ref, wout_ref, o_ref):
    ld = lambda ref: ref[...].astype(F32)
    o_a = _head_norm(ld(oaf_ref) + ld(oab_ref), nwa_ref[...]) * ld(oga_ref)
    o_b = _head_norm(ld(obf_ref) + ld(obb_ref), nwb_ref[...]) * ld(ogb_ref)
    o_c = (ld(hcf_ref) + ld(hcb_ref)) * ld(yc_ref)
    m = ld(ga_ref) * jnp.dot(o_a.astype(BF16), wbr_ref[0], preferred_element_type=F32)
    m = m + ld(gb_ref) * jnp.dot(o_b.astype(BF16), wbr_ref[1], preferred_element_type=F32)
    m = m + ld(gc_ref) * jnp.dot(o_c.astype(BF16), wbr_ref[2], preferred_element_type=F32)
    o_ref[...] = x_ref[...] + jnp.dot(m.astype(BF16), wout_ref[...], preferred_element_type=F32)


def _merge(x, p, oaf, oab, obf, obb, hcf, hcb, nwa, nwb, wbr, wout):
    n_tok = x.shape[0]
    tok = lambda width, offset: pl.BlockSpec((T_MERGE, width), lambda i: (i, offset // width))

    def whole(a):
        return pl.BlockSpec(a.shape, lambda i: (0,) * a.ndim)

    return pl.pallas_call(
        _merge_kernel,
        grid=(n_tok // T_MERGE,),
        in_specs=[tok(D_MODEL, 0), tok(WIDTH, 0), tok(WIDTH, 0), tok(WIDTH, COL_OGA),
                  tok(WIDTH, 0), tok(WIDTH, 0), tok(WIDTH, COL_OGB),
                  tok(WIDTH, 0), tok(WIDTH, 0), tok(WIDTH, COL_YC),
                  tok(D_MODEL, COL_GA), tok(D_MODEL, COL_GB), tok(D_MODEL, COL_GC),
                  whole(nwa), whole(nwb), whole(wbr), whole(wout)],
        out_specs=tok(D_MODEL, 0),
        out_shape=jax.ShapeDtypeStruct((n_tok, D_MODEL), F32),
        compiler_params=pltpu.CompilerParams(
            dimension_semantics=("arbitrary",), vmem_limit_bytes=VMEM_LIMIT),
        name="merge",
    )(x, oaf, oab, p, obf, obb, p, hcf, hcb, p, p, p, p, nwa, nwb, wbr, wout)


def _ffn_kernel(x_ref, xp_ref, xn_ref, nw_ref, wup_ref, cw_ref, cb_ref, wd_ref, fw_ref, o_ref, h_ref,
                slab_ref, *, final_norm):
    i = pl.program_id(0)
    blocks_per_seq = SEQ // T_FFN
    first = (i % blocks_per_seq) == 0
    last = (i % blocks_per_seq) == blocks_per_seq - 1
    nw = nw_ref[...]

    n_slabs = D_MODEL // LANES

    for c in range(n_slabs):
        for s in range(SUBLANES):
            slab_ref[c, s * SLAB_PITCH:s * SLAB_PITCH + FFN_GROUPS, :] = x_ref[
                s * FFN_GROUPS:(s + 1) * FFN_GROUPS, c * LANES:(c + 1) * LANES]

    def gather(start, stride):
        return jnp.concatenate([slab_ref[c, pl.ds(start, SUBLANES, stride=stride), :]
                                for c in range(n_slabs)], axis=1)

    for jj in range(FFN_GROUPS // 2):
        xb = jnp.concatenate([gather(2 * jj, SLAB_PITCH), gather(2 * jj + 1, SLAB_PITCH)], axis=0)
        h_ref[2 * SUBLANES * jj:2 * SUBLANES * (jj + 1), :] = _rms(xb, nw).astype(BF16)
    halo = jnp.concatenate([jnp.where(first, 0.0, _rms(xp_ref[...], nw)),
                            jnp.where(last, 0.0, _rms(xn_ref[...], nw))], axis=0)
    h_ref[T_FFN:, :] = halo.astype(BF16)

    def up(f):
        h = h_ref[...]
        return tuple(jnp.dot(h, wup_ref[:, c * D_FF + f * F_TILE:c * D_FF + (f + 1) * F_TILE],
                             preferred_element_type=F32) for c in range(2))

    def conv(u, c, f):
        cols = slice(c * D_FF + f * F_TILE, c * D_FF + (f + 1) * F_TILE)
        body = u[:T_FFN]
        before = u[T_FFN + SUBLANES - 1:T_FFN + SUBLANES]
        after = u[T_FFN + SUBLANES:T_FFN + SUBLANES + 1]
        prev0 = jnp.concatenate([before, body[T_FFN - SUBLANES:T_FFN - 1]], axis=0)
        next_last = jnp.concatenate([body[1:SUBLANES], after], axis=0)
        u_prev = jnp.concatenate([prev0, body[:T_FFN - SUBLANES]], axis=0)
        u_next = jnp.concatenate([body[SUBLANES:], next_last], axis=0)
        return (cb_ref[:, cols] + u_prev * cw_ref[0:1, cols] + body * cw_ref[1:2, cols]
                + u_next * cw_ref[2:3, cols])

    n_f = D_FF // F_TILE
    acc = None
    pending = up(0)
    for f in range(n_f):
        ug, uv = pending
        if f + 1 < n_f:
            pending = up(f + 1)
        act = _gelu_tanh(conv(ug, 0, f)) * conv(uv, 1, f)
        down = jnp.dot(act.astype(BF16), wd_ref[f * F_TILE:(f + 1) * F_TILE, :],
                       preferred_element_type=F32)
        acc = down if acc is None else acc + down

    for c in range(n_slabs):
        slab_ref[c, 0:T_FFN, :] = acc[:, c * LANES:(c + 1) * LANES]
    groups_per_sublane = FFN_GROUPS // SUBLANES
    for k in range(FFN_GROUPS):
        rows = slice(k * SUBLANES, (k + 1) * SUBLANES)
        s, j0 = k // groups_per_sublane, (k % groups_per_sublane) * SUBLANES
        y = x_ref[rows, :] + gather(j0 * SUBLANES + s, SUBLANES)
        o_ref[rows, :] = _rms(y, fw_ref[...]) if final_norm else y


def _ffn(x, nw, w_up, cw, cb, w_down, fw, final_norm):
    n_tok = x.shape[0]
    halo = T_FFN // SUBLANES
    last_halo = n_tok // SUBLANES - 1

    def resident(a):
        return pl.BlockSpec(a.shape, lambda i: (0,) * a.ndim, pipeline_mode=pl.Buffered(1))

    return pl.pallas_call(
        functools.partial(_ffn_kernel, final_norm=final_norm),
        grid=(n_tok // T_FFN,),
        in_specs=[
            pl.BlockSpec((T_FFN, D_MODEL), lambda i: (i, 0)),
            pl.BlockSpec((SUBLANES, D_MODEL), lambda i: (jnp.maximum(i * halo - 1, 0), 0)),
            pl.BlockSpec((SUBLANES, D_MODEL), lambda i: (jnp.minimum((i + 1) * halo, last_halo), 0)),
            resident(nw), resident(w_up), resident(cw), resident(cb), resident(w_down), resident(fw),
        ],
        out_specs=pl.BlockSpec((T_FFN, D_MODEL), lambda i: (i, 0)),
        out_shape=jax.ShapeDtypeStruct((n_tok, D_MODEL), F32),
        scratch_shapes=[pltpu.VMEM((T_FFN + 2 * SUBLANES, D_MODEL), BF16),
                        pltpu.VMEM((D_MODEL // LANES, SUBLANES * SLAB_PITCH, LANES), F32)],
        compiler_params=pltpu.CompilerParams(
            dimension_semantics=("arbitrary",), vmem_limit_bytes=VMEM_LIMIT),
        name="ffn",
    )(x, x, x, nw, w_up, cw, cb, w_down, fw)


def _layout_w_in(w_in_l):
    (q_a, zf_a, zb_a, i_a, og_a, q_b, k_b, v_b, og_b, lrf, lrb, x_c, y_c, g_a, g_b, g_c) = jnp.split(
        w_in_l, [512, 1024, 1536, 2048, 2560, 2816, 3072, 3584, 4096, 4112, 4128, 4640, 5152, 6176, 7200],
        axis=1)
    pad = jnp.zeros((D_MODEL, LANES - 2 * GLA_RANK), w_in_l.dtype)
    q_b = q_b * GLA_Q_SCALE
    return jnp.concatenate([q_a, zf_a, zb_a, i_a, og_a, q_b, k_b, v_b, og_b, x_c, y_c, g_a, g_b, g_c,
                            lrf, lrb, pad], axis=1).astype(BF16)


def _layout_gla_up(up_w_l):
    out = jnp.zeros((2, LANES, B_KEY), F32)
    out = out.at[0, 0:GLA_RANK].set(up_w_l[0])
    out = out.at[1, GLA_RANK:2 * GLA_RANK].set(up_w_l[1])
    return out.astype(BF16)


def _layout_rglru_gates(wa_l, wx_l):
    def tiles(w):
        w = w.reshape(2, 2, 4, C_BLOCK, C_BLOCK)
        eye = jnp.eye(4, dtype=w.dtype)
        t = jnp.einsum('dhbkj,bc->dhbkcj', w, eye)
        return t.reshape(2, 2, 4 * C_BLOCK, 4 * C_BLOCK)
    return jnp.stack([tiles(wa_l), tiles(wx_l)], axis=1).astype(BF16)


def kernel(x_prompt, x_sample, norm_mix_w, w_in, hgrn_lb_logits, hgrn_norm_w, gla_up_w, gla_up_b,
           gla_norm_w, c_conv_w, c_conv_b, rglru_wa, rglru_ba, rglru_wx, rglru_bx, rglru_lam,
           w_branch, w_out, norm_ffn_w, ffn_up, ffn_conv_w, ffn_conv_b, ffn_down, final_norm_w):
    lb_all = jnp.cumsum(jax.nn.softmax(hgrn_lb_logits.astype(F32), axis=0), axis=0)
    lb_all = lb_all - lb_all[0]
    fw = final_norm_w.reshape(1, D_MODEL)

    xs = [x_prompt.reshape(-1, D_MODEL), x_sample.reshape(-1, D_MODEL)]
    for l in range(DEPTH):
        w_in_l = _layout_w_in(w_in[l])
        up_w_l = _layout_gla_up(gla_up_w[l])
        gate_w_l = _layout_rglru_gates(rglru_wa[l], rglru_wx[l])
        gate_b_l = jnp.stack([rglru_ba[l], rglru_bx[l]], axis=1)
        w_branch_l, w_out_l = w_branch[l].astype(BF16), w_out[l].astype(BF16)
        ffn_up_l, ffn_down_l = ffn_up[l].astype(BF16), ffn_down[l].astype(BF16)
        for n, x in enumerate(xs):
            p = _proj(x, norm_mix_w[l].reshape(1, D_MODEL), lb_all[l], w_in_l)
            oaf, oab = _bidir_gla(p, "hgrn", ())
            obf, obb = _bidir_gla(p, "gla", (up_w_l, gla_up_b[l].reshape(2, 1, B_KEY)))
            hcf, hcb = _rglru(p, c_conv_w[l], c_conv_b[l].reshape(1, WIDTH), gate_w_l, gate_b_l,
                              rglru_lam[l])
            x = _merge(x, p, oaf, oab, obf, obb, hcf, hcb,
                       hgrn_norm_w[l].reshape(1, HEAD_V), gla_norm_w[l].reshape(1, HEAD_V),
                       w_branch_l, w_out_l)
            xs[n] = _ffn(x, norm_ffn_w[l].reshape(1, D_MODEL), ffn_up_l, ffn_conv_w[l],
                         ffn_conv_b[l].reshape(1, 2 * D_FF), ffn_down_l, fw,
                         final_norm=(l == DEPTH - 1))

    return (xs[0].reshape(x_prompt.shape), xs[1].reshape(x_sample.shape))
```

```python
import functools
import math

import jax
import jax.numpy as jnp
from jax import lax
from jax.experimental import pallas as pl
from jax.experimental.pallas import tpu as pltpu

F32 = jnp.float32
BF16 = jnp.bfloat16
P_DTYPE = BF16
O_DTYPE = BF16

D_MODEL = 1024
SEQ = 8192
DEPTH = 4
WIDTH = 512
HEAD_V = 128
A_KEY = 512
B_KEY = 256
GLA_RANK = 16
GLA_NORMALIZER = 16.0
GLA_PAIR = 2
GLA_Q_SCALE = 0.125
C_BLOCKS = 8
C_BLOCK = 64
C_CONV = 4
RG_C = 8.0
D_FF = 2816
FFN_CONV = 3
EPS = 1e-6

LANES = 128
SUBLANES = 8
HALO = 16
VMEM_LIMIT = 56 * 1024 * 1024

N_PROJ = 8320
COL_QA, COL_ZF, COL_ZB, COL_IA, COL_OGA = 0, 512, 1024, 1536, 2048
COL_QB, COL_KB, COL_VB, COL_OGB = 2560, 2816, 3072, 3584
COL_XC, COL_YC = 4096, 4608
COL_GA, COL_GB, COL_GC = 5120, 6144, 7168
COL_LR = 8192

SUB = 16
CHUNK = 64
N_SUB = CHUNK // SUB
PHASE_LAG = 4
T_MIX = 1024
MIX_GROUPS = T_MIX // SUBLANES
MIX_PITCH = MIX_GROUPS + SUBLANES
T_PROJ = 512
N_PROJ_TILE = 2048
T_MERGE = 512
T_FFN = 512
FFN_UP_AHEAD = 2
FFN_GROUPS = T_FFN // SUBLANES
SLAB_PITCH = FFN_GROUPS + SUBLANES
F_TILE = 256


def _sigmoid(x):
    return 1.0 / (1.0 + jnp.exp(-x))


def _sigmoid_tanh(x):
    return 0.5 + 0.5 * jnp.tanh(0.5 * x)


def _gelu_tanh(x):
    return 0.5 * x * (1.0 + jnp.tanh(math.sqrt(2.0 / math.pi) * (x + 0.044715 * (x * x * x))))


def _rms(x, w):
    ms = jnp.mean(x * x, axis=-1, keepdims=True)
    return x * lax.rsqrt(ms + EPS) * w


def _proj_activation(col, u, lb_ref):
    if col == COL_QA:
        return u * _sigmoid_tanh(u) * (HEAD_V ** -0.5)
    if col in (COL_ZF, COL_ZB):
        lb = lb_ref[(col - COL_ZF) // WIDTH:(col - COL_ZF) // WIDTH + 1, :]
        return jnp.log(lb + (1.0 - lb) * _sigmoid(u))
    if col in (COL_OGA, COL_OGB):
        return u * _sigmoid_tanh(u)
    if col == COL_YC:
        return _gelu_tanh(u)
    if COL_GA <= col < COL_LR:
        return _sigmoid_tanh(u)
    return u


def _proj_kernel(x_ref, nw_ref, lb_ref, w_ref, o_ref):
    h = _rms(x_ref[...], nw_ref[...]).astype(BF16)
    for lo in range(0, N_PROJ, N_PROJ_TILE):
        hi = min(lo + N_PROJ_TILE, N_PROJ)
        u = jnp.dot(h, w_ref[:, lo:hi], preferred_element_type=F32)
        for col in range(lo, hi, WIDTH):
            end = min(col + WIDTH, hi)
            o_ref[:, col:end] = _proj_activation(col, u[:, col - lo:end - lo], lb_ref).astype(o_ref.dtype)


def _proj(x, nw, lb, w):
    n_tok = x.shape[0]
    return pl.pallas_call(
        _proj_kernel,
        grid=(n_tok // T_PROJ,),
        in_specs=[
            pl.BlockSpec((T_PROJ, D_MODEL), lambda i: (i, 0)),
            pl.BlockSpec((1, D_MODEL), lambda i: (0, 0), pipeline_mode=pl.Buffered(1)),
            pl.BlockSpec((2, WIDTH), lambda i: (0, 0), pipeline_mode=pl.Buffered(1)),
            pl.BlockSpec((D_MODEL, N_PROJ), lambda i: (0, 0), pipeline_mode=pl.Buffered(1)),
        ],
        out_specs=pl.BlockSpec((T_PROJ, N_PROJ), lambda i: (i, 0)),
        out_shape=jax.ShapeDtypeStruct((n_tok, N_PROJ), P_DTYPE),
        compiler_params=pltpu.CompilerParams(
            dimension_semantics=("arbitrary",), vmem_limit_bytes=VMEM_LIMIT),
        name="proj",
    )(x, nw, lb, w)


def _tri_consts(reverse):
    i = lax.broadcasted_iota(jnp.int32, (CHUNK, CHUNK), 0)
    j = lax.broadcasted_iota(jnp.int32, (CHUNK, CHUNK), 1)
    causal = (j >= i) if reverse else (j <= i)
    same_sub = (i // SUB) == (j // SUB)
    tri = jnp.where(causal, 1.0, 0.0).astype(BF16)
    tri_sub = jnp.where(causal & same_sub, 1.0, 0.0).astype(BF16)
    return jnp.concatenate([tri, tri_sub], axis=0), causal


def _chunk_cumsum(lg, cum_mat):
    lg_hi = lg.astype(BF16)
    lg_lo = (lg - lg_hi.astype(F32)).astype(BF16)
    return jnp.dot(cum_mat, jnp.concatenate([lg_hi, lg_lo], axis=1), preferred_element_type=F32)


def _chunk_scores(cum, q, k, v_list, st_list, head_masks, reverse):
    n_heads = len(v_list)
    b = cum[:CHUNK, :LANES] + cum[:CHUNK, LANES:]
    b_loc = cum[CHUNK:, :LANES] + cum[CHUNK:, LANES:]
    r = b - b_loc
    q_loc = q * jnp.exp(b_loc)
    k_inv = k * jnp.exp(-b_loc)
    q_dec = (q_loc * jnp.exp(r)).astype(BF16)
    last = 0 if reverse else CHUNK - 1
    b_last = b[last:last + 1, :]
    k_tail = (k * jnp.exp(b_last - b)).astype(BF16)
    e_last = jnp.exp(b_last)

    q_loc_bf = q_loc.astype(BF16)
    k_groups = []
    q_groups = []
    for s in range(N_SUB):
        lo, hi = (s * SUB, CHUNK) if reverse else (0, (s + 1) * SUB)
        r_s = r[s * SUB:s * SUB + 1, :]
        kt = (k_inv[lo:hi] * jnp.exp(r_s - r[lo:hi])).astype(BF16)
        pieces = []
        if lo > 0:
            pieces.append(jnp.zeros((lo, LANES), BF16))
        pieces.append(kt)
        if hi < CHUNK:
            pieces.append(jnp.zeros((CHUNK - hi, LANES), BF16))
        k_groups.append(jnp.concatenate(pieces, axis=0) if len(pieces) > 1 else kt)
        qp = []
        if s > 0:
            qp.append(jnp.zeros((s * SUB, LANES), BF16))
        qp.append(q_loc_bf[s * SUB:(s + 1) * SUB])
        if s < N_SUB - 1:
            qp.append(jnp.zeros((CHUNK - (s + 1) * SUB, LANES), BF16))
        q_groups.append(jnp.concatenate(qp, axis=0))
    k_wide = jnp.concatenate(k_groups, axis=1)
    q_wide = jnp.concatenate(q_groups, axis=1)
    if head_masks[0] is None:
        q_stack = q_wide
    else:
        zero = jnp.zeros_like(q_wide)
        q_stack = jnp.concatenate([jnp.where(m[0], q_wide, zero) for m in head_masks], axis=0)
    att = lax.dot_general(q_stack, k_wide, (((1,), (1,)), ((), ())), preferred_element_type=F32)

    o_inter, upd = [], []
    for h in range(n_heads):
        qd = q_dec if head_masks[h] is None else jnp.where(head_masks[h][1], q_dec, jnp.zeros_like(q_dec))
        o_inter.append(lax.dot_general(qd, st_list[h].astype(BF16), (((1,), (1,)), ((), ())),
                                       preferred_element_type=F32))
        upd.append(lax.dot_general(v_list[h], k_tail, (((0,), (0,)), ((), ())),
                                   preferred_element_type=F32))
    return att, o_inter, upd, e_last


def _chunk_outputs(att, o_inter, upd, e_last, v_list, st_list, causal):
    outs, new_states = [], []
    for h in range(len(v_list)):
        att_h = jnp.where(causal, att[h * CHUNK:(h + 1) * CHUNK], 0.0).astype(BF16)
        outs.append(jnp.dot(att_h, v_list[h], preferred_element_type=F32) + o_inter[h])
        new_states.append(st_list[h] * e_last + upd[h])
    return outs, new_states


def _gla_kernel(aqf_ref, azf_ref, avf_ref, aqb_ref, azb_ref, avb_ref,
                bqf_ref, bkf_ref, bvf_ref, blrf_ref, bqb_ref, bkb_ref, bvb_ref, blrb_ref, upw_ref, upb_ref,
                oaf_ref, oab_ref, obf_ref, obb_ref, sta_ref, stb_ref, lg_ref):
    n_heads = WIDTH // HEAD_V

    @pl.when(pl.program_id(1) == 0)
    def _():
        sta_ref[...] = jnp.zeros_like(sta_ref)
        stb_ref[...] = jnp.zeros_like(stb_ref)

    for d, lr_ref in enumerate((blrf_ref, blrb_ref)):
        pre = jnp.dot(lr_ref[...].astype(BF16), upw_ref[d], preferred_element_type=F32) + upb_ref[d]
        log_sig = jnp.minimum(pre, 0.0) - jnp.log(1.0 + jnp.exp(-jnp.abs(pre)))
        lg_ref[d] = log_sig / GLA_NORMALIZER

    consts = [_tri_consts(False), _tri_consts(True)]
    width = LANES // GLA_PAIR
    lane_w = lax.broadcasted_iota(jnp.int32, (CHUNK, N_SUB * LANES), 1) % LANES
    lane = lax.broadcasted_iota(jnp.int32, (CHUNK, LANES), 1)
    pair_masks = [tuple((ln >= h * width) & (ln < (h + 1) * width) for ln in (lane_w, lane))
                  for h in range(GLA_PAIR)]
    n_chunks = T_MIX // CHUNK

    hgrn = [("a", d, g) for g in range(A_KEY // LANES) for d in range(2)]
    gla = [("b", d, g) for g in range(B_KEY // LANES) for d in range(2)]
    chains = []
    while hgrn or gla:
        chains += hgrn[:2] + gla[:1]
        hgrn, gla = hgrn[2:], gla[1:]

    def body(c, carry):
        work = [dict() for _ in chains]

        def phase1(w, mixer, d, g):
            reverse = d == 1
            cc = (n_chunks - 1 - c) if reverse else c
            w["rows"] = rows = pl.ds(pl.multiple_of(cc * CHUNK, CHUNK), CHUNK)
            cols = slice(g * LANES, (g + 1) * LANES)
            if mixer == "a":
                w["q"] = (aqb_ref if reverse else aqf_ref)[rows, cols].astype(F32)
                lg = (azb_ref if reverse else azf_ref)[rows, cols].astype(F32)
                w["k"] = 1.0 - jnp.exp(lg)
            else:
                w["q"] = (bqb_ref if reverse else bqf_ref)[rows, cols].astype(F32)
                w["k"] = (bkb_ref if reverse else bkf_ref)[rows, cols].astype(F32)
                lg = lg_ref[d, rows, cols]
            w["cum"] = _chunk_cumsum(lg, consts[d][0])

        def phase2(w, mixer, d, g):
            if mixer == "a":
                v_ref, st_ref, hs, masks = (avb_ref if d == 1 else avf_ref), sta_ref, [g], [None]
            else:
                v_ref, st_ref = (bvb_ref if d == 1 else bvf_ref), stb_ref
                hs, masks = [g * GLA_PAIR + h for h in range(GLA_PAIR)], pair_masks
            w["hs"] = hs
            w["v"] = [v_ref[w["rows"], h * HEAD_V:(h + 1) * HEAD_V].astype(BF16) for h in hs]
            w["st"] = [st_ref[d * n_heads + h] for h in hs]
            w["scores"] = _chunk_scores(w["cum"], w["q"], w["k"], w["v"], w["st"], masks, d == 1)

        def phase3(w, mixer, d, g):
            if mixer == "a":
                o_ref, st_ref = (oab_ref if d == 1 else oaf_ref), sta_ref
            else:
                o_ref, st_ref = (obb_ref if d == 1 else obf_ref), stb_ref
            outs, new_states = _chunk_outputs(*w["scores"], w["v"], w["st"], consts[d][1])
            for h, o, st in zip(w["hs"], outs, new_states):
                o_ref[w["rows"], h * HEAD_V:(h + 1) * HEAD_V] = o.astype(o_ref.dtype)
                st_ref[d * n_heads + h] = st

        phases = (phase1, phase2, phase3)
        for slot in range(len(chains) + PHASE_LAG * (len(phases) - 1)):
            for p, phase in enumerate(phases):
                n = slot - p * PHASE_LAG
                if 0 <= n < len(chains):
                    phase(work[n], *chains[n])
        return carry

    lax.fori_loop(0, n_chunks, body, 0, unroll=2)


def _bidir_gla(p, upw, upb):
    n_tok = p.shape[0]
    n_seq = n_tok // SEQ
    nb = SEQ // T_MIX
    fwd = lambda s, i: s * nb + i
    bwd = lambda s, i: s * nb + (nb - 1 - i)

    def col(width, offset, blk):
        return pl.BlockSpec((T_MIX, width), lambda s, i: (blk(s, i), offset // width))

    def whole(a):
        return pl.BlockSpec(a.shape, lambda s, i: (0,) * a.ndim)

    in_specs = [col(WIDTH, COL_QA, fwd), col(WIDTH, COL_ZF, fwd), col(WIDTH, COL_IA, fwd),
                col(WIDTH, COL_QA, bwd), col(WIDTH, COL_ZB, bwd), col(WIDTH, COL_IA, bwd),
                col(B_KEY, COL_QB, fwd), col(B_KEY, COL_KB, fwd), col(WIDTH, COL_VB, fwd),
                col(LANES, COL_LR, fwd),
                col(B_KEY, COL_QB, bwd), col(B_KEY, COL_KB, bwd), col(WIDTH, COL_VB, bwd),
                col(LANES, COL_LR, bwd), whole(upw), whole(upb)]
    out_spec_f = pl.BlockSpec((T_MIX, WIDTH), lambda s, i: (fwd(s, i), 0))
    out_spec_b = pl.BlockSpec((T_MIX, WIDTH), lambda s, i: (bwd(s, i), 0))
    n_states = 2 * WIDTH // HEAD_V
    return pl.pallas_call(
        _gla_kernel,
        grid=(n_seq, nb),
        in_specs=in_specs,
        out_specs=[out_spec_f, out_spec_b, out_spec_f, out_spec_b],
        out_shape=[jax.ShapeDtypeStruct((n_tok, WIDTH), O_DTYPE)] * 4,
        scratch_shapes=[pltpu.VMEM((n_states, HEAD_V, LANES), F32), pltpu.VMEM((n_states, HEAD_V, LANES), F32),
                        pltpu.VMEM((2, T_MIX, B_KEY), F32)],
        compiler_params=pltpu.CompilerParams(
            dimension_semantics=("arbitrary", "arbitrary"), vmem_limit_bytes=VMEM_LIMIT),
        name="mix_gla",
    )(*([p] * 14), upw, upb)


def _rglru_kernel(xf_ref, xfp_ref, xfn_ref, xb_ref, xbp_ref, xbn_ref, cw_ref, cb_ref, gw_ref, gb_ref,
                  lam_ref, of_ref, ob_ref, slab_ref, a_ref, u_ref, carry_ref):
    i = pl.program_id(1)
    nb = pl.num_programs(1)

    @pl.when(i == 0)
    def _():
        carry_ref[...] = jnp.zeros_like(carry_ref)

    n_slabs = WIDTH // LANES
    top = T_MIX - SUBLANES

    def gather(start, stride):
        return jnp.concatenate([slab_ref[c, pl.ds(start, SUBLANES, stride=stride), :]
                                for c in range(n_slabs)], axis=1)

    for d in range(2):
        reverse = d == 1
        x_ref, xp_ref, xn_ref, o_ref = (xb_ref, xbp_ref, xbn_ref, ob_ref) if reverse else (
            xf_ref, xfp_ref, xfn_ref, of_ref)
        blk = (nb - 1 - i) if reverse else i
        xf = x_ref[...].astype(F32)
        for c in range(n_slabs):
            for s in range(SUBLANES):
                slab_ref[c, s * MIX_PITCH:s * MIX_PITCH + MIX_GROUPS, :] = xf[
                    s * MIX_GROUPS:(s + 1) * MIX_GROUPS, c * LANES:(c + 1) * LANES]
        groups = [gather(j, MIX_PITCH) for j in range(MIX_GROUPS)]
        x = jnp.concatenate(groups, axis=0)
        before = jnp.where(blk == 0, 0.0, xp_ref[...].astype(F32))
        after = jnp.where(blk == nb - 1, 0.0, xn_ref[...].astype(F32))
        down1 = jnp.concatenate([before[HALO - 1:HALO], groups[-1][:SUBLANES - 1]], axis=0)
        down2 = jnp.concatenate([before[HALO - 2:HALO - 1], groups[-2][:SUBLANES - 1]], axis=0)
        up1 = jnp.concatenate([groups[0][1:], after[0:1]], axis=0)
        x_m1 = jnp.concatenate([down1, x[:top]], axis=0)
        x_m2 = jnp.concatenate([down2, down1, x[:top - SUBLANES]], axis=0)
        x_p1 = jnp.concatenate([x[SUBLANES:], up1], axis=0)
        xc = (cb_ref[...] + x_m2 * cw_ref[0:1, :] + x_m1 * cw_ref[1:2, :] + x * cw_ref[2:3, :]
              + x_p1 * cw_ref[3:4, :])
        xc_bf = xc.astype(BF16)
        gates = []
        for g in range(2):
            halves = [jnp.dot(xc_bf[:, hh * 256:(hh + 1) * 256], gw_ref[d, g, hh],
                              preferred_element_type=F32) for hh in range(2)]
            gates.append(_sigmoid_tanh(jnp.concatenate(halves, axis=1) + gb_ref[d, g:g + 1, :]))
        r_gate, i_gate = gates
        lam = lam_ref[d:d + 1, :]
        softplus_neg_lam = jnp.maximum(-lam, 0.0) + jnp.log(1.0 + jnp.exp(-jnp.abs(lam)))
        log_a = -RG_C * r_gate * softplus_neg_lam
        a = jnp.exp(log_a)
        u = jnp.sqrt(-jnp.tanh(log_a) * (1.0 + a * a)) * (i_gate * xc)
        a_ref[...] = a
        u_ref[...] = u

        def body(n, state):
            h_prev, p_prev = state
            j = (MIX_GROUPS - 1 - n) if reverse else n
            rows = pl.ds(pl.multiple_of(j * SUBLANES, SUBLANES), SUBLANES)
            a_j = a_ref[rows, :]
            h = a_j * h_prev + u_ref[rows, :]
            p = a_j * p_prev
            u_ref[rows, :] = h
            a_ref[rows, :] = p
            return h, p

        init = (jnp.zeros((SUBLANES, WIDTH), F32), jnp.ones((SUBLANES, WIDTH), F32))
        h_end, p_end = lax.fori_loop(0, MIX_GROUPS, body, init, unroll=8)
        carry = carry_ref[d][0:1, :]
        carries = [None] * SUBLANES
        for s in (range(SUBLANES - 1, -1, -1) if reverse else range(SUBLANES)):
            carries[s] = carry
            carry = h_end[s:s + 1, :] + p_end[s:s + 1, :] * carry
        carry_ref[d] = jnp.broadcast_to(carry, (SUBLANES, WIDTH))
        carry_in = jnp.concatenate(carries, axis=0)
        h_all = u_ref[...] + a_ref[...] * jnp.concatenate([carry_in] * MIX_GROUPS, axis=0)
        for c in range(n_slabs):
            slab_ref[c, 0:T_MIX, :] = h_all[:, c * LANES:(c + 1) * LANES]
        groups_per_sublane = MIX_GROUPS // SUBLANES
        for k2 in range(MIX_GROUPS // 2):
            pair = []
            for k in (2 * k2, 2 * k2 + 1):
                s, j0 = k // groups_per_sublane, (k % groups_per_sublane) * SUBLANES
                pair.append(gather(j0 * SUBLANES + s, SUBLANES))
            o_ref[2 * SUBLANES * k2:2 * SUBLANES * (k2 + 1), :] = jnp.concatenate(pair, axis=0).astype(
                o_ref.dtype)


def _rglru(p, cw, cb, gw, gb, lam):
    n_tok = p.shape[0]
    n_seq = n_tok // SEQ
    nb = SEQ // T_MIX
    halo = T_MIX // HALO
    last_halo = n_tok // HALO - 1
    xcol = COL_XC // WIDTH
    fwd = lambda s, i: s * nb + i
    bwd = lambda s, i: s * nb + (nb - 1 - i)

    def specs(blk):
        return [
            pl.BlockSpec((T_MIX, WIDTH), lambda s, i: (blk(s, i), xcol)),
            pl.BlockSpec((HALO, WIDTH), lambda s, i: (jnp.maximum(blk(s, i) * halo - 1, 0), xcol)),
            pl.BlockSpec((HALO, WIDTH),
                         lambda s, i: (jnp.minimum((blk(s, i) + 1) * halo, last_halo), xcol)),
        ]

    def whole(a):
        return pl.BlockSpec(a.shape, lambda s, i: (0,) * a.ndim)

    return pl.pallas_call(
        _rglru_kernel,
        grid=(n_seq, nb),
        in_specs=specs(fwd) + specs(bwd) + [whole(cw), whole(cb), whole(gw), whole(gb), whole(lam)],
        out_specs=[pl.BlockSpec((T_MIX, WIDTH), lambda s, i: (fwd(s, i), 0)),
                   pl.BlockSpec((T_MIX, WIDTH), lambda s, i: (bwd(s, i), 0))],
        out_shape=[jax.ShapeDtypeStruct((n_tok, WIDTH), O_DTYPE)] * 2,
        scratch_shapes=[pltpu.VMEM((WIDTH // LANES, SUBLANES * MIX_PITCH, LANES), F32),
                        pltpu.VMEM((T_MIX, WIDTH), F32),
                        pltpu.VMEM((T_MIX, WIDTH), F32),
                        pltpu.VMEM((2, SUBLANES, WIDTH), F32)],
        compiler_params=pltpu.CompilerParams(
            dimension_semantics=("arbitrary", "arbitrary"), vmem_limit_bytes=VMEM_LIMIT),
        name="mix_rglru",
    )(p, p, p, p, p, p, cw, cb, gw, gb, lam)


def _head_norm(o, w):
    parts = [_rms(o[:, h * HEAD_V:(h + 1) * HEAD_V], w) for h in range(WIDTH // HEAD_V)]
    return jnp.concatenate(parts, axis=1)


def _merge_kernel(x_ref, oaf_ref, oab_ref, oga_ref, obf_ref, obb_ref, ogb_ref, hcf_ref, hcb_ref, yc_ref,
                  ga_ref, gb_ref, gc_ref, nwa_ref, nwb_ref, wbr_ref, wout_ref, o_ref):
    ld = lambda ref: ref[...].astype(F32)
    o_a = _head_norm(ld(oaf_ref) + ld(oab_ref), nwa_ref[...]) * ld(oga_ref)
    o_b = _head_norm(ld(obf_ref) + ld(obb_ref), nwb_ref[...]) * ld(ogb_ref)
    o_c = (ld(hcf_ref) + ld(hcb_ref)) * ld(yc_ref)
    m = ld(ga_ref) * jnp.dot(o_a.astype(BF16), wbr_ref[0], preferred_element_type=F32)
    m = m + ld(gb_ref) * jnp.dot(o_b.astype(BF16), wbr_ref[1], preferred_element_type=F32)
    m = m + ld(gc_ref) * jnp.dot(o_c.astype(BF16), wbr_ref[2], preferred_element_type=F32)
    o_ref[...] = x_ref[...] + jnp.dot(m.astype(BF16), wout_ref[...], preferred_element_type=F32)


def _merge(x, p, oaf, oab, obf, obb, hcf, hcb, nwa, nwb, wbr, wout):
    n_tok = x.shape[0]
    tok = lambda width, offset: pl.BlockSpec((T_MERGE, width), lambda i: (i, offset // width))

    def whole(a):
        return pl.BlockSpec(a.shape, lambda i: (0,) * a.ndim)

    return pl.pallas_call(
        _merge_kernel,
        grid=(n_tok // T_MERGE,),
        in_specs=[tok(D_MODEL, 0), tok(WIDTH, 0), tok(WIDTH, 0), tok(WIDTH, COL_OGA),
                  tok(WIDTH, 0), tok(WIDTH, 0), tok(WIDTH, COL_OGB),
                  tok(WIDTH, 0), tok(WIDTH, 0), tok(WIDTH, COL_YC),
                  tok(D_MODEL, COL_GA), tok(D_MODEL, COL_GB), tok(D_MODEL, COL_GC),
                  whole(nwa), whole(nwb), whole(wbr), whole(wout)],
        out_specs=tok(D_MODEL, 0),
        out_shape=jax.ShapeDtypeStruct((n_tok, D_MODEL), F32),
        compiler_params=pltpu.CompilerParams(
            dimension_semantics=("arbitrary",), vmem_limit_bytes=VMEM_LIMIT),
        name="merge",
    )(x, oaf, oab, p, obf, obb, p, hcf, hcb, p, p, p, p, nwa, nwb, wbr, wout)


def _ffn_kernel(x_ref, xp_ref, xn_ref, nw_ref, wup_ref, cw_ref, cb_ref, wd_ref, fw_ref, o_ref, h_ref,
                slab_ref, *, final_norm):
    i = pl.program_id(0)
    blocks_per_seq = SEQ // T_FFN
    first = (i % blocks_per_seq) == 0
    last = (i % blocks_per_seq) == blocks_per_seq - 1
    nw = nw_ref[...]

    n_slabs = D_MODEL // LANES

    for c in range(n_slabs):
        for s in range(SUBLANES):
            slab_ref[c, s * SLAB_PITCH:s * SLAB_PITCH + FFN_GROUPS, :] = x_ref[
                s * FFN_GROUPS:(s + 1) * FFN_GROUPS, c * LANES:(c + 1) * LANES]

    def gather(start, stride):
        return jnp.concatenate([slab_ref[c, pl.ds(start, SUBLANES, stride=stride), :]
                                for c in range(n_slabs)], axis=1)

    for jj in range(FFN_GROUPS // 2):
        xb = jnp.concatenate([gather(2 * jj, SLAB_PITCH), gather(2 * jj + 1, SLAB_PITCH)], axis=0)
        h_ref[2 * SUBLANES * jj:2 * SUBLANES * (jj + 1), :] = _rms(xb, nw).astype(BF16)
    halo = jnp.concatenate([jnp.where(first, 0.0, _rms(xp_ref[...], nw)),
                            jnp.where(last, 0.0, _rms(xn_ref[...], nw))], axis=0)
    h_ref[T_FFN:, :] = halo.astype(BF16)

    def up(f):
        h = h_ref[...]
        return tuple(jnp.dot(h, wup_ref[:, c * D_FF + f * F_TILE:c * D_FF + (f + 1) * F_TILE],
                             preferred_element_type=F32) for c in range(2))

    def conv(u, c, f):
        cols = slice(c * D_FF + f * F_TILE, c * D_FF + (f + 1) * F_TILE)
        body = u[:T_FFN]
        before = u[T_FFN + SUBLANES - 1:T_FFN + SUBLANES]
        after = u[T_FFN + SUBLANES:T_FFN + SUBLANES + 1]
        prev0 = jnp.concatenate([before, body[T_FFN - SUBLANES:T_FFN - 1]], axis=0)
        next_last = jnp.concatenate([body[1:SUBLANES], after], axis=0)
        u_prev = jnp.concatenate([prev0, body[:T_FFN - SUBLANES]], axis=0)
        u_next = jnp.concatenate([body[SUBLANES:], next_last], axis=0)
        return (cb_ref[:, cols] + u_prev * cw_ref[0:1, cols] + body * cw_ref[1:2, cols]
                + u_next * cw_ref[2:3, cols])

    n_f = D_FF // F_TILE
    acc = None
    pending = [up(f) for f in range(FFN_UP_AHEAD)]
    for f in range(n_f):
        ug, uv = pending.pop(0)
        if f + FFN_UP_AHEAD < n_f:
            pending.append(up(f + FFN_UP_AHEAD))
        act = _gelu_tanh(conv(ug, 0, f)) * conv(uv, 1, f)
        down = jnp.dot(act.astype(BF16), wd_ref[f * F_TILE:(f + 1) * F_TILE, :],
                       preferred_element_type=F32)
        acc = down if acc is None else acc + down

    for c in range(n_slabs):
        slab_ref[c, 0:T_FFN, :] = acc[:, c * LANES:(c + 1) * LANES]
    groups_per_sublane = FFN_GROUPS // SUBLANES
    for k in range(FFN_GROUPS):
        rows = slice(k * SUBLANES, (k + 1) * SUBLANES)
        s, j0 = k // groups_per_sublane, (k % groups_per_sublane) * SUBLANES
        y = x_ref[rows, :] + gather(j0 * SUBLANES + s, SUBLANES)
        o_ref[rows, :] = _rms(y, fw_ref[...]) if final_norm else y


def _ffn(x, nw, w_up, cw, cb, w_down, fw, final_norm):
    n_tok = x.shape[0]
    halo = T_FFN // SUBLANES
    last_halo = n_tok // SUBLANES - 1

    def resident(a):
        return pl.BlockSpec(a.shape, lambda i: (0,) * a.ndim, pipeline_mode=pl.Buffered(1))

    return pl.pallas_call(
        functools.partial(_ffn_kernel, final_norm=final_norm),
        grid=(n_tok // T_FFN,),
        in_specs=[
            pl.BlockSpec((T_FFN, D_MODEL), lambda i: (i, 0)),
            pl.BlockSpec((SUBLANES, D_MODEL), lambda i: (jnp.maximum(i * halo - 1, 0), 0)),
            pl.BlockSpec((SUBLANES, D_MODEL), lambda i: (jnp.minimum((i + 1) * halo, last_halo), 0)),
            resident(nw), resident(w_up), resident(cw), resident(cb), resident(w_down), resident(fw),
        ],
        out_specs=pl.BlockSpec((T_FFN, D_MODEL), lambda i: (i, 0)),
        out_shape=jax.ShapeDtypeStruct((n_tok, D_MODEL), F32),
        scratch_shapes=[pltpu.VMEM((T_FFN + 2 * SUBLANES, D_MODEL), BF16),
                        pltpu.VMEM((D_MODEL // LANES, SUBLANES * SLAB_PITCH, LANES), F32)],
        compiler_params=pltpu.CompilerParams(
            dimension_semantics=("arbitrary",), vmem_limit_bytes=VMEM_LIMIT),
        name="ffn",
    )(x, x, x, nw, w_up, cw, cb, w_down, fw)


def _layout_w_in(w_in_l):
    (q_a, zf_a, zb_a, i_a, og_a, q_b, k_b, v_b, og_b, lrf, lrb, x_c, y_c, g_a, g_b, g_c) = jnp.split(
        w_in_l, [512, 1024, 1536, 2048, 2560, 2816, 3072, 3584, 4096, 4112, 4128, 4640, 5152, 6176, 7200],
        axis=1)
    pad = jnp.zeros((D_MODEL, LANES - 2 * GLA_RANK), w_in_l.dtype)
    q_b = q_b * GLA_Q_SCALE
    return jnp.concatenate([q_a, zf_a, zb_a, i_a, og_a, q_b, k_b, v_b, og_b, x_c, y_c, g_a, g_b, g_c,
                            lrf, lrb, pad], axis=1).astype(BF16)


def _layout_gla_up(up_w_l):
    out = jnp.zeros((2, LANES, B_KEY), F32)
    out = out.at[0, 0:GLA_RANK].set(up_w_l[0])
    out = out.at[1, GLA_RANK:2 * GLA_RANK].set(up_w_l[1])
    return out.astype(BF16)


def _layout_rglru_gates(wa_l, wx_l):
    def tiles(w):
        w = w.reshape(2, 2, 4, C_BLOCK, C_BLOCK)
        eye = jnp.eye(4, dtype=w.dtype)
        t = jnp.einsum('dhbkj,bc->dhbkcj', w, eye)
        return t.reshape(2, 2, 4 * C_BLOCK, 4 * C_BLOCK)
    return jnp.stack([tiles(wa_l), tiles(wx_l)], axis=1).astype(BF16)


def kernel(x_prompt, x_sample, norm_mix_w, w_in, hgrn_lb_logits, hgrn_norm_w, gla_up_w, gla_up_b,
           gla_norm_w, c_conv_w, c_conv_b, rglru_wa, rglru_ba, rglru_wx, rglru_bx, rglru_lam,
           w_branch, w_out, norm_ffn_w, ffn_up, ffn_conv_w, ffn_conv_b, ffn_down, final_norm_w):
    lb_all = jnp.cumsum(jax.nn.softmax(hgrn_lb_logits.astype(F32), axis=0), axis=0)
    lb_all = lb_all - lb_all[0]
    fw = final_norm_w.reshape(1, D_MODEL)

    xs = [x_prompt.reshape(-1, D_MODEL), x_sample.reshape(-1, D_MODEL)]
    for l in range(DEPTH):
        w_in_l = _layout_w_in(w_in[l])
        up_w_l = _layout_gla_up(gla_up_w[l])
        gate_w_l = _layout_rglru_gates(rglru_wa[l], rglru_wx[l])
        gate_b_l = jnp.stack([rglru_ba[l], rglru_bx[l]], axis=1)
        w_branch_l, w_out_l = w_branch[l].astype(BF16), w_out[l].astype(BF16)
        ffn_up_l, ffn_down_l = ffn_up[l].astype(BF16), ffn_down[l].astype(BF16)
        for n, x in enumerate(xs):
            p = _proj(x, norm_mix_w[l].reshape(1, D_MODEL), lb_all[l], w_in_l)
            oaf, oab, obf, obb = _bidir_gla(p, up_w_l, gla_up_b[l].reshape(2, 1, B_KEY))
            hcf, hcb = _rglru(p, c_conv_w[l], c_conv_b[l].reshape(1, WIDTH), gate_w_l, gate_b_l,
                              rglru_lam[l])
            x = _merge(x, p, oaf, oab, obf, obb, hcf, hcb,
                       hgrn_norm_w[l].reshape(1, HEAD_V), gla_norm_w[l].reshape(1, HEAD_V),
                       w_branch_l, w_out_l)
            xs[n] = _ffn(x, norm_ffn_w[l].reshape(1, D_MODEL), ffn_up_l, ffn_conv_w[l],
                         ffn_conv_b[l].reshape(1, 2 * D_FF), ffn_down_l, fw,
                         final_norm=(l == DEPTH - 1))

    return (xs[0].reshape(x_prompt.shape), xs[1].reshape(x_sample.shape))
```

```python
import functools
import math

import jax
import jax.numpy as jnp
from jax import lax
from jax.experimental import pallas as pl
from jax.experimental.pallas import tpu as pltpu

F32 = jnp.float32
BF16 = jnp.bfloat16
P_DTYPE = BF16
O_DTYPE = BF16

D_MODEL = 1024
SEQ = 8192
DEPTH = 4
WIDTH = 512
HEAD_V = 128
A_KEY = 512
B_KEY = 256
GLA_RANK = 16
GLA_NORMALIZER = 16.0
GLA_PAIR = 2
GLA_Q_SCALE = 0.125
C_BLOCKS = 8
C_BLOCK = 64
C_CONV = 4
RG_C = 8.0
D_FF = 2816
FFN_CONV = 3
EPS = 1e-6

LANES = 128
SUBLANES = 8
HALO = 16
VMEM_LIMIT = 56 * 1024 * 1024

N_PROJ = 8320
COL_QA, COL_ZF, COL_ZB, COL_IA, COL_OGA = 0, 512, 1024, 1536, 2048
COL_QB, COL_KB, COL_VB, COL_OGB = 2560, 2816, 3072, 3584
COL_XC, COL_YC = 4096, 4608
COL_GA, COL_GB, COL_GC = 5120, 6144, 7168
COL_LR = 8192

SUB = 16
CHUNK = 64
N_SUB = CHUNK // SUB
PHASE_LAG = 4
T_MIX = 1024
MIX_GROUPS = T_MIX // SUBLANES
MIX_PITCH = MIX_GROUPS + SUBLANES
T_PROJ = 512
N_PROJ_TILE = 2048
T_MERGE = 512
T_FFN = 512
FFN_UP_AHEAD = 2
FFN_GROUPS = T_FFN // SUBLANES
SLAB_PITCH = FFN_GROUPS + SUBLANES
F_TILE = 256


def _sigmoid(x):
    return 1.0 / (1.0 + jnp.exp(-x))


def _sigmoid_tanh(x):
    return 0.5 + 0.5 * jnp.tanh(0.5 * x)


def _gelu_tanh(x):
    return 0.5 * x * (1.0 + jnp.tanh(math.sqrt(2.0 / math.pi) * (x + 0.044715 * (x * x * x))))


def _rms(x, w):
    ms = jnp.mean(x * x, axis=-1, keepdims=True)
    return x * lax.rsqrt(ms + EPS) * w


def _proj_activation(col, u, lb_ref):
    if col == COL_QA:
        return u * _sigmoid_tanh(u) * (HEAD_V ** -0.5)
    if col in (COL_ZF, COL_ZB):
        lb = lb_ref[(col - COL_ZF) // WIDTH:(col - COL_ZF) // WIDTH + 1, :]
        return jnp.log(lb + (1.0 - lb) * _sigmoid(u))
    if col in (COL_OGA, COL_OGB):
        return u * _sigmoid_tanh(u)
    if col == COL_YC:
        return _gelu_tanh(u)
    if COL_GA <= col < COL_LR:
        return _sigmoid_tanh(u)
    return u


def _proj_kernel(x_ref, nw_ref, lb_ref, w_ref, o_ref):
    h = _rms(x_ref[...], nw_ref[...]).astype(BF16)
    for lo in range(0, N_PROJ, N_PROJ_TILE):
        hi = min(lo + N_PROJ_TILE, N_PROJ)
        u = jnp.dot(h, w_ref[:, lo:hi], preferred_element_type=F32)
        for col in range(lo, hi, WIDTH):
            end = min(col + WIDTH, hi)
            o_ref[:, col:end] = _proj_activation(col, u[:, col - lo:end - lo], lb_ref).astype(o_ref.dtype)


def _proj(x, nw, lb, w):
    n_tok = x.shape[0]
    return pl.pallas_call(
        _proj_kernel,
        grid=(n_tok // T_PROJ,),
        in_specs=[
            pl.BlockSpec((T_PROJ, D_MODEL), lambda i: (i, 0)),
            pl.BlockSpec((1, D_MODEL), lambda i: (0, 0), pipeline_mode=pl.Buffered(1)),
            pl.BlockSpec((2, WIDTH), lambda i: (0, 0), pipeline_mode=pl.Buffered(1)),
            pl.BlockSpec((D_MODEL, N_PROJ), lambda i: (0, 0), pipeline_mode=pl.Buffered(1)),
        ],
        out_specs=pl.BlockSpec((T_PROJ, N_PROJ), lambda i: (i, 0)),
        out_shape=jax.ShapeDtypeStruct((n_tok, N_PROJ), P_DTYPE),
        compiler_params=pltpu.CompilerParams(
            dimension_semantics=("arbitrary",), vmem_limit_bytes=VMEM_LIMIT),
        name="proj",
    )(x, nw, lb, w)


def _tri_consts(reverse):
    i = lax.broadcasted_iota(jnp.int32, (CHUNK, CHUNK), 0)
    j = lax.broadcasted_iota(jnp.int32, (CHUNK, CHUNK), 1)
    causal = (j >= i) if reverse else (j <= i)
    same_sub = (i // SUB) == (j // SUB)
    return jnp.where(causal & same_sub, 1.0, 0.0).astype(BF16), causal


def _chunk_cumsum(lg, cum_mat):
    lg_hi = lg.astype(BF16)
    lg_lo = (lg - lg_hi.astype(F32)).astype(BF16)
    return jnp.dot(cum_mat, jnp.concatenate([lg_hi, lg_lo], axis=1), preferred_element_type=F32)


def _chunk_scores(cum, q, k, v_list, st_list, head_masks, reverse):
    n_heads = len(v_list)
    b_loc = cum[:, :LANES] + cum[:, LANES:]
    order = range(N_SUB - 1, -1, -1) if reverse else range(N_SUB)
    offsets, total = [None] * N_SUB, jnp.zeros((1, LANES), F32)
    for s in order:
        offsets[s] = total
        edge = s * SUB if reverse else (s + 1) * SUB - 1
        total = total + b_loc[edge:edge + 1, :]
    r = jnp.concatenate([jnp.broadcast_to(o, (SUB, LANES)) for o in offsets], axis=0)
    b = b_loc + r
    q_loc = q * jnp.exp(b_loc)
    k_inv = k * jnp.exp(-b_loc)
    q_dec = (q_loc * jnp.exp(r)).astype(BF16)
    last = 0 if reverse else CHUNK - 1
    b_last = b[last:last + 1, :]
    k_tail = (k * jnp.exp(b_last - b)).astype(BF16)
    e_last = jnp.exp(b_last)

    q_loc_bf = q_loc.astype(BF16)
    k_groups = []
    q_groups = []
    for s in range(N_SUB):
        lo, hi = (s * SUB, CHUNK) if reverse else (0, (s + 1) * SUB)
        r_s = r[s * SUB:s * SUB + 1, :]
        kt = (k_inv[lo:hi] * jnp.exp(r_s - r[lo:hi])).astype(BF16)
        pieces = []
        if lo > 0:
            pieces.append(jnp.zeros((lo, LANES), BF16))
        pieces.append(kt)
        if hi < CHUNK:
            pieces.append(jnp.zeros((CHUNK - hi, LANES), BF16))
        k_groups.append(jnp.concatenate(pieces, axis=0) if len(pieces) > 1 else kt)
        qp = []
        if s > 0:
            qp.append(jnp.zeros((s * SUB, LANES), BF16))
        qp.append(q_loc_bf[s * SUB:(s + 1) * SUB])
        if s < N_SUB - 1:
            qp.append(jnp.zeros((CHUNK - (s + 1) * SUB, LANES), BF16))
        q_groups.append(jnp.concatenate(qp, axis=0))
    k_wide = jnp.concatenate(k_groups, axis=1)
    q_wide = jnp.concatenate(q_groups, axis=1)
    if head_masks[0] is None:
        q_stack = q_wide
    else:
        zero = jnp.zeros_like(q_wide)
        q_stack = jnp.concatenate([jnp.where(m[0], q_wide, zero) for m in head_masks], axis=0)
    att = lax.dot_general(q_stack, k_wide, (((1,), (1,)), ((), ())), preferred_element_type=F32)

    o_inter, upd = [], []
    for h in range(n_heads):
        qd = q_dec if head_masks[h] is None else jnp.where(head_masks[h][1], q_dec, jnp.zeros_like(q_dec))
        o_inter.append(lax.dot_general(qd, st_list[h].astype(BF16), (((1,), (1,)), ((), ())),
                                       preferred_element_type=F32))
        upd.append(lax.dot_general(v_list[h], k_tail, (((0,), (0,)), ((), ())),
                                   preferred_element_type=F32))
    return att, o_inter, upd, e_last


def _chunk_outputs(att, o_inter, upd, e_last, v_list, st_list, causal):
    outs, new_states = [], []
    for h in range(len(v_list)):
        att_h = jnp.where(causal, att[h * CHUNK:(h + 1) * CHUNK], 0.0).astype(BF16)
        outs.append(jnp.dot(att_h, v_list[h], preferred_element_type=F32) + o_inter[h])
        new_states.append(st_list[h] * e_last + upd[h])
    return outs, new_states


def _gla_kernel(aqf_ref, azf_ref, avf_ref, aqb_ref, azb_ref, avb_ref,
                bqf_ref, bkf_ref, bvf_ref, blrf_ref, bqb_ref, bkb_ref, bvb_ref, blrb_ref, upw_ref, upb_ref,
                oaf_ref, oab_ref, obf_ref, obb_ref, sta_ref, stb_ref, lg_ref):
    n_heads = WIDTH // HEAD_V

    @pl.when(pl.program_id(1) == 0)
    def _():
        sta_ref[...] = jnp.zeros_like(sta_ref)
        stb_ref[...] = jnp.zeros_like(stb_ref)

    for d, lr_ref in enumerate((blrf_ref, blrb_ref)):
        pre = jnp.dot(lr_ref[...].astype(BF16), upw_ref[d], preferred_element_type=F32) + upb_ref[d]
        log_sig = jnp.minimum(pre, 0.0) - jnp.log(1.0 + jnp.exp(-jnp.abs(pre)))
        lg_ref[d] = log_sig / GLA_NORMALIZER

    consts = [_tri_consts(False), _tri_consts(True)]
    width = LANES // GLA_PAIR
    lane_w = lax.broadcasted_iota(jnp.int32, (CHUNK, N_SUB * LANES), 1) % LANES
    lane = lax.broadcasted_iota(jnp.int32, (CHUNK, LANES), 1)
    pair_masks = [tuple((ln >= h * width) & (ln < (h + 1) * width) for ln in (lane_w, lane))
                  for h in range(GLA_PAIR)]
    n_chunks = T_MIX // CHUNK

    hgrn = [("a", d, g) for g in range(A_KEY // LANES) for d in range(2)]
    gla = [("b", d, g) for g in range(B_KEY // LANES) for d in range(2)]
    chains = []
    while hgrn or gla:
        chains += hgrn[:2] + gla[:1]
        hgrn, gla = hgrn[2:], gla[1:]

    def body(c, carry):
        work = [dict() for _ in chains]

        def phase1(w, mixer, d, g):
            reverse = d == 1
            cc = (n_chunks - 1 - c) if reverse else c
            w["rows"] = rows = pl.ds(pl.multiple_of(cc * CHUNK, CHUNK), CHUNK)
            cols = slice(g * LANES, (g + 1) * LANES)
            if mixer == "a":
                w["q"] = (aqb_ref if reverse else aqf_ref)[rows, cols].astype(F32)
                lg = (azb_ref if reverse else azf_ref)[rows, cols].astype(F32)
                w["k"] = 1.0 - jnp.exp(lg)
            else:
                w["q"] = (bqb_ref if reverse else bqf_ref)[rows, cols].astype(F32)
                w["k"] = (bkb_ref if reverse else bkf_ref)[rows, cols].astype(F32)
                lg = lg_ref[d, rows, cols]
            w["cum"] = _chunk_cumsum(lg, consts[d][0])

        def phase2(w, mixer, d, g):
            if mixer == "a":
                v_ref, st_ref, hs, masks = (avb_ref if d == 1 else avf_ref), sta_ref, [g], [None]
            else:
                v_ref, st_ref = (bvb_ref if d == 1 else bvf_ref), stb_ref
                hs, masks = [g * GLA_PAIR + h for h in range(GLA_PAIR)], pair_masks
            w["hs"] = hs
            w["v"] = [v_ref[w["rows"], h * HEAD_V:(h + 1) * HEAD_V].astype(BF16) for h in hs]
            w["st"] = [st_ref[d * n_heads + h] for h in hs]
            w["scores"] = _chunk_scores(w["cum"], w["q"], w["k"], w["v"], w["st"], masks, d == 1)

        def phase3(w, mixer, d, g):
            if mixer == "a":
                o_ref, st_ref = (oab_ref if d == 1 else oaf_ref), sta_ref
            else:
                o_ref, st_ref = (obb_ref if d == 1 else obf_ref), stb_ref
            outs, new_states = _chunk_outputs(*w["scores"], w["v"], w["st"], consts[d][1])
            for h, o, st in zip(w["hs"], outs, new_states):
                o_ref[w["rows"], h * HEAD_V:(h + 1) * HEAD_V] = o.astype(o_ref.dtype)
                st_ref[d * n_heads + h] = st

        phases = (phase1, phase2, phase3)
        for slot in range(len(chains) + PHASE_LAG * (len(phases) - 1)):
            for p, phase in enumerate(phases):
                n = slot - p * PHASE_LAG
                if 0 <= n < len(chains):
                    phase(work[n], *chains[n])
        return carry

    lax.fori_loop(0, n_chunks, body, 0, unroll=4)


def _bidir_gla(p, upw, upb):
    n_tok = p.shape[0]
    n_seq = n_tok // SEQ
    nb = SEQ // T_MIX
    fwd = lambda s, i: s * nb + i
    bwd = lambda s, i: s * nb + (nb - 1 - i)

    def col(width, offset, blk):
        return pl.BlockSpec((T_MIX, width), lambda s, i: (blk(s, i), offset // width))

    def whole(a):
        return pl.BlockSpec(a.shape, lambda s, i: (0,) * a.ndim)

    in_specs = [col(WIDTH, COL_QA, fwd), col(WIDTH, COL_ZF, fwd), col(WIDTH, COL_IA, fwd),
                col(WIDTH, COL_QA, bwd), col(WIDTH, COL_ZB, bwd), col(WIDTH, COL_IA, bwd),
                col(B_KEY, COL_QB, fwd), col(B_KEY, COL_KB, fwd), col(WIDTH, COL_VB, fwd),
                col(LANES, COL_LR, fwd),
                col(B_KEY, COL_QB, bwd), col(B_KEY, COL_KB, bwd), col(WIDTH, COL_VB, bwd),
                col(LANES, COL_LR, bwd), whole(upw), whole(upb)]
    out_spec_f = pl.BlockSpec((T_MIX, WIDTH), lambda s, i: (fwd(s, i), 0))
    out_spec_b = pl.BlockSpec((T_MIX, WIDTH), lambda s, i: (bwd(s, i), 0))
    n_states = 2 * WIDTH // HEAD_V
    return pl.pallas_call(
        _gla_kernel,
        grid=(n_seq, nb),
        in_specs=in_specs,
        out_specs=[out_spec_f, out_spec_b, out_spec_f, out_spec_b],
        out_shape=[jax.ShapeDtypeStruct((n_tok, WIDTH), O_DTYPE)] * 4,
        scratch_shapes=[pltpu.VMEM((n_states, HEAD_V, LANES), F32), pltpu.VMEM((n_states, HEAD_V, LANES), F32),
                        pltpu.VMEM((2, T_MIX, B_KEY), F32)],
        compiler_params=pltpu.CompilerParams(
            dimension_semantics=("arbitrary", "arbitrary"), vmem_limit_bytes=VMEM_LIMIT),
        name="mix_gla",
    )(*([p] * 14), upw, upb)


def _rglru_kernel(xf_ref, xfp_ref, xfn_ref, xb_ref, xbp_ref, xbn_ref, cw_ref, cb_ref, gw_ref, gb_ref,
                  lam_ref, of_ref, ob_ref, slab_ref, a_ref, u_ref, carry_ref):
    i = pl.program_id(1)
    nb = pl.num_programs(1)

    @pl.when(i == 0)
    def _():
        carry_ref[...] = jnp.zeros_like(carry_ref)

    n_slabs = WIDTH // LANES
    top = T_MIX - SUBLANES

    def gather(start, stride):
        return jnp.concatenate([slab_ref[c, pl.ds(start, SUBLANES, stride=stride), :]
                                for c in range(n_slabs)], axis=1)

    for d in range(2):
        reverse = d == 1
        x_ref, xp_ref, xn_ref, o_ref = (xb_ref, xbp_ref, xbn_ref, ob_ref) if reverse else (
            xf_ref, xfp_ref, xfn_ref, of_ref)
        blk = (nb - 1 - i) if reverse else i
        xf = x_ref[...].astype(F32)
        for c in range(n_slabs):
            for s in range(SUBLANES):
                slab_ref[c, s * MIX_PITCH:s * MIX_PITCH + MIX_GROUPS, :] = xf[
                    s * MIX_GROUPS:(s + 1) * MIX_GROUPS, c * LANES:(c + 1) * LANES]
        groups = [gather(j, MIX_PITCH) for j in range(MIX_GROUPS)]
        x = jnp.concatenate(groups, axis=0)
        before = jnp.where(blk == 0, 0.0, xp_ref[...].astype(F32))
        after = jnp.where(blk == nb - 1, 0.0, xn_ref[...].astype(F32))
        down1 = jnp.concatenate([before[HALO - 1:HALO], groups[-1][:SUBLANES - 1]], axis=0)
        down2 = jnp.concatenate([before[HALO - 2:HALO - 1], groups[-2][:SUBLANES - 1]], axis=0)
        up1 = jnp.concatenate([groups[0][1:], after[0:1]], axis=0)
        x_m1 = jnp.concatenate([down1, x[:top]], axis=0)
        x_m2 = jnp.concatenate([down2, down1, x[:top - SUBLANES]], axis=0)
        x_p1 = jnp.concatenate([x[SUBLANES:], up1], axis=0)
        xc = (cb_ref[...] + x_m2 * cw_ref[0:1, :] + x_m1 * cw_ref[1:2, :] + x * cw_ref[2:3, :]
              + x_p1 * cw_ref[3:4, :])
        xc_bf = xc.astype(BF16)
        gates = []
        for g in range(2):
            halves = [jnp.dot(xc_bf[:, hh * 256:(hh + 1) * 256], gw_ref[d, g, hh],
                              preferred_element_type=F32) for hh in range(2)]
            gates.append(_sigmoid_tanh(jnp.concatenate(halves, axis=1) + gb_ref[d, g:g + 1, :]))
        r_gate, i_gate = gates
        lam = lam_ref[d:d + 1, :]
        softplus_neg_lam = jnp.maximum(-lam, 0.0) + jnp.log(1.0 + jnp.exp(-jnp.abs(lam)))
        log_a = -RG_C * r_gate * softplus_neg_lam
        a = jnp.exp(log_a)
        one_minus_a2 = -jnp.tanh(log_a) * (1.0 + a * a)
        root = jnp.where(one_minus_a2 > 0.0, one_minus_a2 * lax.rsqrt(one_minus_a2), 0.0)
        u = root * (i_gate * xc)
        a_ref[...] = a
        u_ref[...] = u

        def body(n, state):
            h_prev, p_prev = state
            j = (MIX_GROUPS - 1 - n) if reverse else n
            rows = pl.ds(pl.multiple_of(j * SUBLANES, SUBLANES), SUBLANES)
            a_j = a_ref[rows, :]
            h = a_j * h_prev + u_ref[rows, :]
            p = a_j * p_prev
            u_ref[rows, :] = h
            a_ref[rows, :] = p
            return h, p

        init = (jnp.zeros((SUBLANES, WIDTH), F32), jnp.ones((SUBLANES, WIDTH), F32))
        h_end, p_end = lax.fori_loop(0, MIX_GROUPS, body, init, unroll=8)
        carry = carry_ref[d][0:1, :]
        carries = [None] * SUBLANES
        for s in (range(SUBLANES - 1, -1, -1) if reverse else range(SUBLANES)):
            carries[s] = carry
            carry = h_end[s:s + 1, :] + p_end[s:s + 1, :] * carry
        carry_ref[d] = jnp.broadcast_to(carry, (SUBLANES, WIDTH))
        carry_in = jnp.concatenate(carries, axis=0)
        h_all = u_ref[...] + a_ref[...] * jnp.concatenate([carry_in] * MIX_GROUPS, axis=0)
        for c in range(n_slabs):
            slab_ref[c, 0:T_MIX, :] = h_all[:, c * LANES:(c + 1) * LANES]
        groups_per_sublane = MIX_GROUPS // SUBLANES
        for k2 in range(MIX_GROUPS // 2):
            pair = []
            for k in (2 * k2, 2 * k2 + 1):
                s, j0 = k // groups_per_sublane, (k % groups_per_sublane) * SUBLANES
                pair.append(gather(j0 * SUBLANES + s, SUBLANES))
            o_ref[2 * SUBLANES * k2:2 * SUBLANES * (k2 + 1), :] = jnp.concatenate(pair, axis=0).astype(
                o_ref.dtype)


def _rglru(p, cw, cb, gw, gb, lam):
    n_tok = p.shape[0]
    n_seq = n_tok // SEQ
    nb = SEQ // T_MIX
    halo = T_MIX // HALO
    last_halo = n_tok // HALO - 1
    xcol = COL_XC // WIDTH
    fwd = lambda s, i: s * nb + i
    bwd = lambda s, i: s * nb + (nb - 1 - i)

    def specs(blk):
        return [
            pl.BlockSpec((T_MIX, WIDTH), lambda s, i: (blk(s, i), xcol)),
            pl.BlockSpec((HALO, WIDTH), lambda s, i: (jnp.maximum(blk(s, i) * halo - 1, 0), xcol)),
            pl.BlockSpec((HALO, WIDTH),
                         lambda s, i: (jnp.minimum((blk(s, i) + 1) * halo, last_halo), xcol)),
        ]

    def whole(a):
        return pl.BlockSpec(a.shape, lambda s, i: (0,) * a.ndim)

    return pl.pallas_call(
        _rglru_kernel,
        grid=(n_seq, nb),
        in_specs=specs(fwd) + specs(bwd) + [whole(cw), whole(cb), whole(gw), whole(gb), whole(lam)],
        out_specs=[pl.BlockSpec((T_MIX, WIDTH), lambda s, i: (fwd(s, i), 0)),
                   pl.BlockSpec((T_MIX, WIDTH), lambda s, i: (bwd(s, i), 0))],
        out_shape=[jax.ShapeDtypeStruct((n_tok, WIDTH), O_DTYPE)] * 2,
        scratch_shapes=[pltpu.VMEM((WIDTH // LANES, SUBLANES * MIX_PITCH, LANES), F32),
                        pltpu.VMEM((T_MIX, WIDTH), F32),
                        pltpu.VMEM((T_MIX, WIDTH), F32),
                        pltpu.VMEM((2, SUBLANES, WIDTH), F32)],
        compiler_params=pltpu.CompilerParams(
            dimension_semantics=("arbitrary", "arbitrary"), vmem_limit_bytes=VMEM_LIMIT),
        name="mix_rglru",
    )(p, p, p, p, p, p, cw, cb, gw, gb, lam)


def _head_norm(o, w):
    parts = [_rms(o[:, h * HEAD_V:(h + 1) * HEAD_V], w) for h in range(WIDTH // HEAD_V)]
    return jnp.concatenate(parts, axis=1)


def _merge_kernel(x_ref, oaf_ref, oab_ref, oga_ref, obf_ref, obb_ref, ogb_ref, hcf_ref, hcb_ref, yc_ref,
                  ga_ref, gb_ref, gc_ref, nwa_ref, nwb_ref, wbr_ref, wout_ref, o_ref):
    ld = lambda ref: ref[...].astype(F32)
    o_a = _head_norm(ld(oaf_ref) + ld(oab_ref), nwa_ref[...]) * ld(oga_ref)
    o_b = _head_norm(ld(obf_ref) + ld(obb_ref), nwb_ref[...]) * ld(ogb_ref)
    o_c = (ld(hcf_ref) + ld(hcb_ref)) * ld(yc_ref)
    m = ld(ga_ref) * jnp.dot(o_a.astype(BF16), wbr_ref[0], preferred_element_type=F32)
    m = m + ld(gb_ref) * jnp.dot(o_b.astype(BF16), wbr_ref[1], preferred_element_type=F32)
    m = m + ld(gc_ref) * jnp.dot(o_c.astype(BF16), wbr_ref[2], preferred_element_type=F32)
    o_ref[...] = x_ref[...] + jnp.dot(m.astype(BF16), wout_ref[...], preferred_element_type=F32)


def _merge(x, p, oaf, oab, obf, obb, hcf, hcb, nwa, nwb, wbr, wout):
    n_tok = x.shape[0]
    tok = lambda width, offset: pl.BlockSpec((T_MERGE, width), lambda i: (i, offset // width))

    def whole(a):
        return pl.BlockSpec(a.shape, lambda i: (0,) * a.ndim)

    return pl.pallas_call(
        _merge_kernel,
        grid=(n_tok // T_MERGE,),
        in_specs=[tok(D_MODEL, 0), tok(WIDTH, 0), tok(WIDTH, 0), tok(WIDTH, COL_OGA),
                  tok(WIDTH, 0), tok(WIDTH, 0), tok(WIDTH, COL_OGB),
                  tok(WIDTH, 0), tok(WIDTH, 0), tok(WIDTH, COL_YC),
                  tok(D_MODEL, COL_GA), tok(D_MODEL, COL_GB), tok(D_MODEL, COL_GC),
                  whole(nwa), whole(nwb), whole(wbr), whole(wout)],
        out_specs=tok(D_MODEL, 0),
        out_shape=jax.ShapeDtypeStruct((n_tok, D_MODEL), F32),
        compiler_params=pltpu.CompilerParams(
            dimension_semantics=("arbitrary",), vmem_limit_bytes=VMEM_LIMIT),
        name="merge",
    )(x, oaf, oab, p, obf, obb, p, hcf, hcb, p, p, p, p, nwa, nwb, wbr, wout)


def _ffn_kernel(x_ref, xp_ref, xn_ref, nw_ref, wup_ref, cw_ref, cb_ref, wd_ref, fw_ref, o_ref, h_ref,
                slab_ref, *, final_norm):
    i = pl.program_id(0)
    blocks_per_seq = SEQ // T_FFN
    first = (i % blocks_per_seq) == 0
    last = (i % blocks_per_seq) == blocks_per_seq - 1
    nw = nw_ref[...]

    n_slabs = D_MODEL // LANES

    for c in range(n_slabs):
        for s in range(SUBLANES):
            slab_ref[c, s * SLAB_PITCH:s * SLAB_PITCH + FFN_GROUPS, :] = x_ref[
                s * FFN_GROUPS:(s + 1) * FFN_GROUPS, c * LANES:(c + 1) * LANES]

    def gather(start, stride):
        return jnp.concatenate([slab_ref[c, pl.ds(start, SUBLANES, stride=stride), :]
                                for c in range(n_slabs)], axis=1)

    for jj in range(FFN_GROUPS // 2):
        xb = jnp.concatenate([gather(2 * jj, SLAB_PITCH), gather(2 * jj + 1, SLAB_PITCH)], axis=0)
        h_ref[2 * SUBLANES * jj:2 * SUBLANES * (jj + 1), :] = _rms(xb, nw).astype(BF16)
    halo = jnp.concatenate([jnp.where(first, 0.0, _rms(xp_ref[...], nw)),
                            jnp.where(last, 0.0, _rms(xn_ref[...], nw))], axis=0)
    h_ref[T_FFN:, :] = halo.astype(BF16)

    def up(f):
        h = h_ref[...]
        return tuple(jnp.dot(h, wup_ref[:, c * D_FF + f * F_TILE:c * D_FF + (f + 1) * F_TILE],
                             preferred_element_type=F32) for c in range(2))

    def conv(u, c, f):
        cols = slice(c * D_FF + f * F_TILE, c * D_FF + (f + 1) * F_TILE)
        body = u[:T_FFN]
        before = u[T_FFN + SUBLANES - 1:T_FFN + SUBLANES]
        after = u[T_FFN + SUBLANES:T_FFN + SUBLANES + 1]
        prev0 = jnp.concatenate([before, body[T_FFN - SUBLANES:T_FFN - 1]], axis=0)
        next_last = jnp.concatenate([body[1:SUBLANES], after], axis=0)
        u_prev = jnp.concatenate([prev0, body[:T_FFN - SUBLANES]], axis=0)
        u_next = jnp.concatenate([body[SUBLANES:], next_last], axis=0)
        return (cb_ref[:, cols] + u_prev * cw_ref[0:1, cols] + body * cw_ref[1:2, cols]
                + u_next * cw_ref[2:3, cols])

    n_f = D_FF // F_TILE
    acc = None
    pending = [up(f) for f in range(FFN_UP_AHEAD)]
    for f in range(n_f):
        ug, uv = pending.pop(0)
        if f + FFN_UP_AHEAD < n_f:
            pending.append(up(f + FFN_UP_AHEAD))
        act = _gelu_tanh(conv(ug, 0, f)) * conv(uv, 1, f)
        down = jnp.dot(act.astype(BF16), wd_ref[f * F_TILE:(f + 1) * F_TILE, :],
                       preferred_element_type=F32)
        acc = down if acc is None else acc + down

    for c in range(n_slabs):
        slab_ref[c, 0:T_FFN, :] = acc[:, c * LANES:(c + 1) * LANES]
    groups_per_sublane = FFN_GROUPS // SUBLANES
    for k in range(FFN_GROUPS):
        rows = slice(k * SUBLANES, (k + 1) * SUBLANES)
        s, j0 = k // groups_per_sublane, (k % groups_per_sublane) * SUBLANES
        y = x_ref[rows, :] + gather(j0 * SUBLANES + s, SUBLANES)
        o_ref[rows, :] = _rms(y, fw_ref[...]) if final_norm else y


def _ffn(x, nw, w_up, cw, cb, w_down, fw, final_norm):
    n_tok = x.shape[0]
    halo = T_FFN // SUBLANES
    last_halo = n_tok // SUBLANES - 1

    def resident(a):
        return pl.BlockSpec(a.shape, lambda i: (0,) * a.ndim, pipeline_mode=pl.Buffered(1))

    return pl.pallas_call(
        functools.partial(_ffn_kernel, final_norm=final_norm),
        grid=(n_tok // T_FFN,),
        in_specs=[
            pl.BlockSpec((T_FFN, D_MODEL), lambda i: (i, 0)),
            pl.BlockSpec((SUBLANES, D_MODEL), lambda i: (jnp.maximum(i * halo - 1, 0), 0)),
            pl.BlockSpec((SUBLANES, D_MODEL), lambda i: (jnp.minimum((i + 1) * halo, last_halo), 0)),
            resident(nw), resident(w_up), resident(cw), resident(cb), resident(w_down), resident(fw),
        ],
        out_specs=pl.BlockSpec((T_FFN, D_MODEL), lambda i: (i, 0)),
        out_shape=jax.ShapeDtypeStruct((n_tok, D_MODEL), F32),
        scratch_shapes=[pltpu.VMEM((T_FFN + 2 * SUBLANES, D_MODEL), BF16),
                        pltpu.VMEM((D_MODEL // LANES, SUBLANES * SLAB_PITCH, LANES), F32)],
        compiler_params=pltpu.CompilerParams(
            dimension_semantics=("arbitrary",), vmem_limit_bytes=VMEM_LIMIT),
        name="ffn",
    )(x, x, x, nw, w_up, cw, cb, w_down, fw)


def _layout_w_in(w_in_l):
    (q_a, zf_a, zb_a, i_a, og_a, q_b, k_b, v_b, og_b, lrf, lrb, x_c, y_c, g_a, g_b, g_c) = jnp.split(
        w_in_l.astype(BF16),
        [512, 1024, 1536, 2048, 2560, 2816, 3072, 3584, 4096, 4112, 4128, 4640, 5152, 6176, 7200], axis=1)
    pad = jnp.zeros((D_MODEL, LANES - 2 * GLA_RANK), BF16)
    q_b = q_b * GLA_Q_SCALE
    return jnp.concatenate([q_a, zf_a, zb_a, i_a, og_a, q_b, k_b, v_b, og_b, x_c, y_c, g_a, g_b, g_c,
                            lrf, lrb, pad], axis=1)


def _layout_gla_up(up_w_l):
    out = jnp.zeros((2, LANES, B_KEY), F32)
    out = out.at[0, 0:GLA_RANK].set(up_w_l[0])
    out = out.at[1, GLA_RANK:2 * GLA_RANK].set(up_w_l[1])
    return out.astype(BF16)


def _layout_rglru_gates(wa_l, wx_l):
    def tiles(w):
        w = w.reshape(2, 2, 4, C_BLOCK, C_BLOCK)
        eye = jnp.eye(4, dtype=w.dtype)
        t = jnp.einsum('dhbkj,bc->dhbkcj', w, eye)
        return t.reshape(2, 2, 4 * C_BLOCK, 4 * C_BLOCK)
    return jnp.stack([tiles(wa_l), tiles(wx_l)], axis=1).astype(BF16)


def kernel(x_prompt, x_sample, norm_mix_w, w_in, hgrn_lb_logits, hgrn_norm_w, gla_up_w, gla_up_b,
           gla_norm_w, c_conv_w, c_conv_b, rglru_wa, rglru_ba, rglru_wx, rglru_bx, rglru_lam,
           w_branch, w_out, norm_ffn_w, ffn_up, ffn_conv_w, ffn_conv_b, ffn_down, final_norm_w):
    lb_all = jnp.cumsum(jax.nn.softmax(hgrn_lb_logits.astype(F32), axis=0), axis=0)
    lb_all = lb_all - lb_all[0]
    fw = final_norm_w.reshape(1, D_MODEL)

    xs = [x_prompt.reshape(-1, D_MODEL), x_sample.reshape(-1, D_MODEL)]
    for l in range(DEPTH):
        w_in_l = _layout_w_in(w_in[l])
        up_w_l = _layout_gla_up(gla_up_w[l])
        gate_w_l = _layout_rglru_gates(rglru_wa[l], rglru_wx[l])
        gate_b_l = jnp.stack([rglru_ba[l], rglru_bx[l]], axis=1)
        w_branch_l, w_out_l = w_branch[l].astype(BF16), w_out[l].astype(BF16)
        ffn_up_l, ffn_down_l = ffn_up[l].astype(BF16), ffn_down[l].astype(BF16)
        for n, x in enumerate(xs):
            p = _proj(x, norm_mix_w[l].reshape(1, D_MODEL), lb_all[l], w_in_l)
            oaf, oab, obf, obb = _bidir_gla(p, up_w_l, gla_up_b[l].reshape(2, 1, B_KEY))
            hcf, hcb = _rglru(p, c_conv_w[l], c_conv_b[l].reshape(1, WIDTH), gate_w_l, gate_b_l,
                              rglru_lam[l])
            x = _merge(x, p, oaf, oab, obf, obb, hcf, hcb,
                       hgrn_norm_w[l].reshape(1, HEAD_V), gla_norm_w[l].reshape(1, HEAD_V),
                       w_branch_l, w_out_l)
            xs[n] = _ffn(x, norm_ffn_w[l].reshape(1, D_MODEL), ffn_up_l, ffn_conv_w[l],
                         ffn_conv_b[l].reshape(1, 2 * D_FF), ffn_down_l, fw,
                         final_norm=(l == DEPTH - 1))

    return (xs[0].reshape(x_prompt.shape), xs[1].reshape(x_sample.shape))
```

```python
import functools
import math

import jax
import jax.numpy as jnp
from jax import lax
from jax.experimental import pallas as pl
from jax.experimental.pallas import tpu as pltpu

F32 = jnp.float32
BF16 = jnp.bfloat16
P_DTYPE = BF16
O_DTYPE = BF16

D_MODEL = 1024
SEQ = 8192
DEPTH = 4
WIDTH = 512
HEAD_V = 128
A_KEY = 512
B_KEY = 256
GLA_RANK = 16
GLA_NORMALIZER = 16.0
GLA_PAIR = 2
GLA_Q_SCALE = 0.125
C_BLOCK = 64
C_CONV = 4
RG_C = 8.0
D_FF = 2816
FFN_CONV = 3
EPS = 1e-6

LANES = 128
SUBLANES = 8
HALO = 16
VMEM_LIMIT = 56 * 1024 * 1024

N_PROJ = 8320
COL_QA, COL_ZF, COL_ZB, COL_IA, COL_OGA = 0, 512, 1024, 1536, 2048
COL_QB, COL_KB, COL_VB, COL_OGB = 2560, 2816, 3072, 3584
COL_XC, COL_YC = 4096, 4608
COL_GA, COL_GB, COL_GC = 5120, 6144, 7168
COL_LR = 8192

SUB = 16
CHUNK = 64
N_SUB = CHUNK // SUB
PHASE_LAG = 4
T_MIX = 1024
MIX_GROUPS = T_MIX // SUBLANES
MIX_PITCH = MIX_GROUPS + SUBLANES
T_PROJ = 512
N_PROJ_TILE = 2048
T_MERGE = 512
T_FFN = 512
FFN_UP_AHEAD = 2
FFN_GROUPS = T_FFN // SUBLANES
SLAB_PITCH = FFN_GROUPS + SUBLANES
F_TILE = 256


def _sigmoid(x):
    return 1.0 / (1.0 + jnp.exp(-x))


def _sigmoid_tanh(x):
    return 0.5 + 0.5 * jnp.tanh(0.5 * x)


def _gelu_tanh(x):
    return 0.5 * x * (1.0 + jnp.tanh(math.sqrt(2.0 / math.pi) * (x + 0.044715 * (x * x * x))))


def _rms(x, w):
    ms = jnp.mean(x * x, axis=-1, keepdims=True)
    return x * lax.rsqrt(ms + EPS) * w


def _proj_activation(col, u, lb_ref):
    if col == COL_QA:
        return u * _sigmoid_tanh(u) * (HEAD_V ** -0.5)
    if col in (COL_ZF, COL_ZB):
        lb = lb_ref[(col - COL_ZF) // WIDTH:(col - COL_ZF) // WIDTH + 1, :]
        return jnp.log(lb + (1.0 - lb) * _sigmoid(u))
    if col in (COL_OGA, COL_OGB):
        return u * _sigmoid_tanh(u)
    if col == COL_YC:
        return _gelu_tanh(u)
    if COL_GA <= col < COL_LR:
        return _sigmoid_tanh(u)
    return u


def _proj_kernel(x_ref, nw_ref, lb_ref, w_ref, o_ref):
    h = _rms(x_ref[...], nw_ref[...]).astype(BF16)
    for lo in range(0, N_PROJ, N_PROJ_TILE):
        hi = min(lo + N_PROJ_TILE, N_PROJ)
        u = jnp.dot(h, w_ref[:, lo:hi], preferred_element_type=F32)
        for col in range(lo, hi, WIDTH):
            end = min(col + WIDTH, hi)
            o_ref[:, col:end] = _proj_activation(col, u[:, col - lo:end - lo], lb_ref).astype(o_ref.dtype)


def _proj(x, nw, lb, w):
    n_tok = x.shape[0]
    return pl.pallas_call(
        _proj_kernel,
        grid=(n_tok // T_PROJ,),
        in_specs=[
            pl.BlockSpec((T_PROJ, D_MODEL), lambda i: (i, 0)),
            pl.BlockSpec((1, D_MODEL), lambda i: (0, 0), pipeline_mode=pl.Buffered(1)),
            pl.BlockSpec((2, WIDTH), lambda i: (0, 0), pipeline_mode=pl.Buffered(1)),
            pl.BlockSpec((D_MODEL, N_PROJ), lambda i: (0, 0), pipeline_mode=pl.Buffered(1)),
        ],
        out_specs=pl.BlockSpec((T_PROJ, N_PROJ), lambda i: (i, 0)),
        out_shape=jax.ShapeDtypeStruct((n_tok, N_PROJ), P_DTYPE),
        compiler_params=pltpu.CompilerParams(
            dimension_semantics=("arbitrary",), vmem_limit_bytes=VMEM_LIMIT),
        name="proj",
    )(x, nw, lb, w)


def _tri_consts(reverse):
    i = lax.broadcasted_iota(jnp.int32, (CHUNK, CHUNK), 0)
    j = lax.broadcasted_iota(jnp.int32, (CHUNK, CHUNK), 1)
    causal = (j >= i) if reverse else (j <= i)
    same_sub = (i // SUB) == (j // SUB)
    return jnp.where(causal & same_sub, 1.0, 0.0).astype(BF16), causal


def _chunk_cumsum(lg, cum_mat):
    lg_hi = lg.astype(BF16)
    lg_lo = (lg - lg_hi.astype(F32)).astype(BF16)
    return jnp.dot(cum_mat, jnp.concatenate([lg_hi, lg_lo], axis=1), preferred_element_type=F32)


def _chunk_scores(cum, q, k, v_list, st_list, head_masks, reverse):
    n_heads = len(v_list)
    b_loc = cum[:, :LANES] + cum[:, LANES:]
    order = range(N_SUB - 1, -1, -1) if reverse else range(N_SUB)
    offsets, total = [None] * N_SUB, jnp.zeros((1, LANES), F32)
    for s in order:
        offsets[s] = total
        edge = s * SUB if reverse else (s + 1) * SUB - 1
        total = total + b_loc[edge:edge + 1, :]
    r = jnp.concatenate([jnp.broadcast_to(o, (SUB, LANES)) for o in offsets], axis=0)
    b = b_loc + r
    q_loc = q * jnp.exp(b_loc)
    k_inv = k * jnp.exp(-b_loc)
    q_dec = (q_loc * jnp.exp(r)).astype(BF16)
    last = 0 if reverse else CHUNK - 1
    b_last = b[last:last + 1, :]
    k_tail = (k * jnp.exp(b_last - b)).astype(BF16)
    e_last = jnp.exp(b_last)

    q_loc_bf = q_loc.astype(BF16)
    k_groups = []
    q_groups = []
    for s in range(N_SUB):
        lo, hi = (s * SUB, CHUNK) if reverse else (0, (s + 1) * SUB)
        r_s = r[s * SUB:s * SUB + 1, :]
        kt = (k_inv[lo:hi] * jnp.exp(r_s - r[lo:hi])).astype(BF16)
        pieces = []
        if lo > 0:
            pieces.append(jnp.zeros((lo, LANES), BF16))
        pieces.append(kt)
        if hi < CHUNK:
            pieces.append(jnp.zeros((CHUNK - hi, LANES), BF16))
        k_groups.append(jnp.concatenate(pieces, axis=0) if len(pieces) > 1 else kt)
        qp = []
        if s > 0:
            qp.append(jnp.zeros((s * SUB, LANES), BF16))
        qp.append(q_loc_bf[s * SUB:(s + 1) * SUB])
        if s < N_SUB - 1:
            qp.append(jnp.zeros((CHUNK - (s + 1) * SUB, LANES), BF16))
        q_groups.append(jnp.concatenate(qp, axis=0))
    k_wide = jnp.concatenate(k_groups, axis=1)
    q_wide = jnp.concatenate(q_groups, axis=1)
    if head_masks[0] is None:
        q_stack = q_wide
    else:
        zero = jnp.zeros_like(q_wide)
        q_stack = jnp.concatenate([jnp.where(m[0], q_wide, zero) for m in head_masks], axis=0)
    att = lax.dot_general(q_stack, k_wide, (((1,), (1,)), ((), ())), preferred_element_type=F32)

    o_inter, upd = [], []
    for h in range(n_heads):
        qd = q_dec if head_masks[h] is None else jnp.where(head_masks[h][1], q_dec, jnp.zeros_like(q_dec))
        o_inter.append(lax.dot_general(qd, st_list[h].astype(BF16), (((1,), (1,)), ((), ())),
                                       preferred_element_type=F32))
        upd.append(lax.dot_general(v_list[h], k_tail, (((0,), (0,)), ((), ())),
                                   preferred_element_type=F32))
    return att, o_inter, upd, e_last


def _chunk_outputs(att, o_inter, upd, e_last, v_list, st_list, causal):
    outs, new_states = [], []
    for h in range(len(v_list)):
        att_h = jnp.where(causal, att[h * CHUNK:(h + 1) * CHUNK], 0.0).astype(BF16)
        outs.append(jnp.dot(att_h, v_list[h], preferred_element_type=F32) + o_inter[h])
        new_states.append(st_list[h] * e_last + upd[h])
    return outs, new_states


def _gla_kernel(aqf_ref, azf_ref, avf_ref, aqb_ref, azb_ref, avb_ref,
                bqf_ref, bkf_ref, bvf_ref, blrf_ref, bqb_ref, bkb_ref, bvb_ref, blrb_ref, upw_ref, upb_ref,
                oaf_ref, oab_ref, obf_ref, obb_ref, sta_ref, stb_ref, lg_ref):
    n_heads = WIDTH // HEAD_V

    @pl.when(pl.program_id(1) == 0)
    def _():
        sta_ref[...] = jnp.zeros_like(sta_ref)
        stb_ref[...] = jnp.zeros_like(stb_ref)

    for d, lr_ref in enumerate((blrf_ref, blrb_ref)):
        pre = jnp.dot(lr_ref[...].astype(BF16), upw_ref[d], preferred_element_type=F32) + upb_ref[d]
        log_sig = jnp.minimum(pre, 0.0) - jnp.log(1.0 + jnp.exp(-jnp.abs(pre)))
        lg_ref[d] = log_sig / GLA_NORMALIZER

    consts = [_tri_consts(False), _tri_consts(True)]
    width = LANES // GLA_PAIR
    lane_w = lax.broadcasted_iota(jnp.int32, (CHUNK, N_SUB * LANES), 1) % LANES
    lane = lax.broadcasted_iota(jnp.int32, (CHUNK, LANES), 1)
    pair_masks = [tuple((ln >= h * width) & (ln < (h + 1) * width) for ln in (lane_w, lane))
                  for h in range(GLA_PAIR)]
    n_chunks = T_MIX // CHUNK

    hgrn = [("a", d, g) for g in range(A_KEY // LANES) for d in range(2)]
    gla = [("b", d, g) for g in range(B_KEY // LANES) for d in range(2)]
    chains = []
    while hgrn or gla:
        chains += hgrn[:2] + gla[:1]
        hgrn, gla = hgrn[2:], gla[1:]

    def body(c, carry):
        work = [dict() for _ in chains]

        def phase1(w, mixer, d, g):
            reverse = d == 1
            cc = (n_chunks - 1 - c) if reverse else c
            w["rows"] = rows = pl.ds(pl.multiple_of(cc * CHUNK, CHUNK), CHUNK)
            cols = slice(g * LANES, (g + 1) * LANES)
            if mixer == "a":
                w["q"] = (aqb_ref if reverse else aqf_ref)[rows, cols].astype(F32)
                lg = (azb_ref if reverse else azf_ref)[rows, cols].astype(F32)
                w["k"] = 1.0 - jnp.exp(lg)
            else:
                w["q"] = (bqb_ref if reverse else bqf_ref)[rows, cols].astype(F32)
                w["k"] = (bkb_ref if reverse else bkf_ref)[rows, cols].astype(F32)
                lg = lg_ref[d, rows, cols]
            w["cum"] = _chunk_cumsum(lg, consts[d][0])

        def phase2(w, mixer, d, g):
            if mixer == "a":
                v_ref, st_ref, hs, masks = (avb_ref if d == 1 else avf_ref), sta_ref, [g], [None]
            else:
                v_ref, st_ref = (bvb_ref if d == 1 else bvf_ref), stb_ref
                hs, masks = [g * GLA_PAIR + h for h in range(GLA_PAIR)], pair_masks
            w["hs"] = hs
            w["v"] = [v_ref[w["rows"], h * HEAD_V:(h + 1) * HEAD_V].astype(BF16) for h in hs]
            w["st"] = [st_ref[d * n_heads + h] for h in hs]
            w["scores"] = _chunk_scores(w["cum"], w["q"], w["k"], w["v"], w["st"], masks, d == 1)

        def phase3(w, mixer, d, g):
            if mixer == "a":
                o_ref, st_ref = (oab_ref if d == 1 else oaf_ref), sta_ref
            else:
                o_ref, st_ref = (obb_ref if d == 1 else obf_ref), stb_ref
            outs, new_states = _chunk_outputs(*w["scores"], w["v"], w["st"], consts[d][1])
            for h, o, st in zip(w["hs"], outs, new_states):
                o_ref[w["rows"], h * HEAD_V:(h + 1) * HEAD_V] = o.astype(o_ref.dtype)
                st_ref[d * n_heads + h] = st

        phases = (phase1, phase2, phase3)
        for slot in range(len(chains) + PHASE_LAG * (len(phases) - 1)):
            for p, phase in enumerate(phases):
                n = slot - p * PHASE_LAG
                if 0 <= n < len(chains):
                    phase(work[n], *chains[n])
        return carry

    lax.fori_loop(0, n_chunks, body, 0, unroll=8)


def _bidir_gla(p, upw, upb):
    n_tok = p.shape[0]
    n_seq = n_tok // SEQ
    nb = SEQ // T_MIX
    fwd = lambda s, i: s * nb + i
    bwd = lambda s, i: s * nb + (nb - 1 - i)

    def col(width, offset, blk):
        return pl.BlockSpec((T_MIX, width), lambda s, i: (blk(s, i), offset // width))

    def whole(a):
        return pl.BlockSpec(a.shape, lambda s, i: (0,) * a.ndim)

    in_specs = [col(WIDTH, COL_QA, fwd), col(WIDTH, COL_ZF, fwd), col(WIDTH, COL_IA, fwd),
                col(WIDTH, COL_QA, bwd), col(WIDTH, COL_ZB, bwd), col(WIDTH, COL_IA, bwd),
                col(B_KEY, COL_QB, fwd), col(B_KEY, COL_KB, fwd), col(WIDTH, COL_VB, fwd),
                col(LANES, COL_LR, fwd),
                col(B_KEY, COL_QB, bwd), col(B_KEY, COL_KB, bwd), col(WIDTH, COL_VB, bwd),
                col(LANES, COL_LR, bwd), whole(upw), whole(upb)]
    out_spec_f = pl.BlockSpec((T_MIX, WIDTH), lambda s, i: (fwd(s, i), 0))
    out_spec_b = pl.BlockSpec((T_MIX, WIDTH), lambda s, i: (bwd(s, i), 0))
    n_states = 2 * WIDTH // HEAD_V
    return pl.pallas_call(
        _gla_kernel,
        grid=(n_seq, nb),
        in_specs=in_specs,
        out_specs=[out_spec_f, out_spec_b, out_spec_f, out_spec_b],
        out_shape=[jax.ShapeDtypeStruct((n_tok, WIDTH), O_DTYPE)] * 4,
        scratch_shapes=[pltpu.VMEM((n_states, HEAD_V, LANES), F32), pltpu.VMEM((n_states, HEAD_V, LANES), F32),
                        pltpu.VMEM((2, T_MIX, B_KEY), F32)],
        compiler_params=pltpu.CompilerParams(
            dimension_semantics=("arbitrary", "arbitrary"), vmem_limit_bytes=VMEM_LIMIT),
        name="mix_gla",
    )(*([p] * 14), upw, upb)


def _rglru_kernel(xf_ref, xfp_ref, xfn_ref, xb_ref, xbp_ref, xbn_ref, cw_ref, cb_ref, gw_ref, gb_ref,
                  lam_ref, of_ref, ob_ref, slab_ref, a_ref, u_ref, carry_ref):
    i = pl.program_id(1)
    nb = pl.num_programs(1)

    @pl.when(i == 0)
    def _():
        carry_ref[...] = jnp.zeros_like(carry_ref)

    n_slabs = WIDTH // LANES
    top = T_MIX - SUBLANES

    def gather(start, stride):
        return jnp.concatenate([slab_ref[c, pl.ds(start, SUBLANES, stride=stride), :]
                                for c in range(n_slabs)], axis=1)

    for d in range(2):
        reverse = d == 1
        x_ref, xp_ref, xn_ref, o_ref = (xb_ref, xbp_ref, xbn_ref, ob_ref) if reverse else (
            xf_ref, xfp_ref, xfn_ref, of_ref)
        blk = (nb - 1 - i) if reverse else i
        xf = x_ref[...].astype(F32)
        for c in range(n_slabs):
            for s in range(SUBLANES):
                slab_ref[c, s * MIX_PITCH:s * MIX_PITCH + MIX_GROUPS, :] = xf[
                    s * MIX_GROUPS:(s + 1) * MIX_GROUPS, c * LANES:(c + 1) * LANES]
        groups = [gather(j, MIX_PITCH) for j in range(MIX_GROUPS)]
        x = jnp.concatenate(groups, axis=0)
        before = jnp.where(blk == 0, 0.0, xp_ref[...].astype(F32))
        after = jnp.where(blk == nb - 1, 0.0, xn_ref[...].astype(F32))
        down1 = jnp.concatenate([before[HALO - 1:HALO], groups[-1][:SUBLANES - 1]], axis=0)
        down2 = jnp.concatenate([before[HALO - 2:HALO - 1], groups[-2][:SUBLANES - 1]], axis=0)
        up1 = jnp.concatenate([groups[0][1:], after[0:1]], axis=0)
        x_m1 = jnp.concatenate([down1, x[:top]], axis=0)
        x_m2 = jnp.concatenate([down2, down1, x[:top - SUBLANES]], axis=0)
        x_p1 = jnp.concatenate([x[SUBLANES:], up1], axis=0)
        xc = (cb_ref[...] + x_m2 * cw_ref[0:1, :] + x_m1 * cw_ref[1:2, :] + x * cw_ref[2:3, :]
              + x_p1 * cw_ref[3:4, :])
        xc_bf = xc.astype(BF16)
        gates = []
        for g in range(2):
            halves = [jnp.dot(xc_bf[:, hh * 256:(hh + 1) * 256], gw_ref[d, g, hh],
                              preferred_element_type=F32) for hh in range(2)]
            gates.append(_sigmoid_tanh(jnp.concatenate(halves, axis=1) + gb_ref[d, g:g + 1, :]))
        r_gate, i_gate = gates
        lam = lam_ref[d:d + 1, :]
        softplus_neg_lam = jnp.maximum(-lam, 0.0) + jnp.log(1.0 + jnp.exp(-jnp.abs(lam)))
        log_a = -RG_C * r_gate * softplus_neg_lam
        a = jnp.exp(log_a)
        one_minus_a2 = -jnp.tanh(log_a) * (1.0 + a * a)
        root = jnp.where(one_minus_a2 > 0.0, one_minus_a2 * lax.rsqrt(one_minus_a2), 0.0)
        u = root * (i_gate * xc)
        a_ref[...] = a
        u_ref[...] = u

        def body(n, state):
            h_prev, p_prev = state
            j = (MIX_GROUPS - 1 - n) if reverse else n
            rows = pl.ds(pl.multiple_of(j * SUBLANES, SUBLANES), SUBLANES)
            a_j = a_ref[rows, :]
            h = a_j * h_prev + u_ref[rows, :]
            p = a_j * p_prev
            u_ref[rows, :] = h
            a_ref[rows, :] = p
            return h, p

        init = (jnp.zeros((SUBLANES, WIDTH), F32), jnp.ones((SUBLANES, WIDTH), F32))
        h_end, p_end = lax.fori_loop(0, MIX_GROUPS, body, init, unroll=8)
        carry = carry_ref[d][0:1, :]
        carries = [None] * SUBLANES
        for s in (range(SUBLANES - 1, -1, -1) if reverse else range(SUBLANES)):
            carries[s] = carry
            carry = h_end[s:s + 1, :] + p_end[s:s + 1, :] * carry
        carry_ref[d] = jnp.broadcast_to(carry, (SUBLANES, WIDTH))
        carry_in = jnp.concatenate(carries, axis=0)
        h_all = u_ref[...] + a_ref[...] * jnp.concatenate([carry_in] * MIX_GROUPS, axis=0)
        for c in range(n_slabs):
            slab_ref[c, 0:T_MIX, :] = h_all[:, c * LANES:(c + 1) * LANES]
        groups_per_sublane = MIX_GROUPS // SUBLANES
        for k2 in range(MIX_GROUPS // 2):
            pair = []
            for k in (2 * k2, 2 * k2 + 1):
                s, j0 = k // groups_per_sublane, (k % groups_per_sublane) * SUBLANES
                pair.append(gather(j0 * SUBLANES + s, SUBLANES))
            o_ref[2 * SUBLANES * k2:2 * SUBLANES * (k2 + 1), :] = jnp.concatenate(pair, axis=0).astype(
                o_ref.dtype)


def _rglru(p, cw, cb, gw, gb, lam):
    n_tok = p.shape[0]
    n_seq = n_tok // SEQ
    nb = SEQ // T_MIX
    halo = T_MIX // HALO
    last_halo = n_tok // HALO - 1
    xcol = COL_XC // WIDTH
    fwd = lambda s, i: s * nb + i
    bwd = lambda s, i: s * nb + (nb - 1 - i)

    def specs(blk):
        return [
            pl.BlockSpec((T_MIX, WIDTH), lambda s, i: (blk(s, i), xcol)),
            pl.BlockSpec((HALO, WIDTH), lambda s, i: (jnp.maximum(blk(s, i) * halo - 1, 0), xcol)),
            pl.BlockSpec((HALO, WIDTH),
                         lambda s, i: (jnp.minimum((blk(s, i) + 1) * halo, last_halo), xcol)),
        ]

    def whole(a):
        return pl.BlockSpec(a.shape, lambda s, i: (0,) * a.ndim)

    return pl.pallas_call(
        _rglru_kernel,
        grid=(n_seq, nb),
        in_specs=specs(fwd) + specs(bwd) + [whole(cw), whole(cb), whole(gw), whole(gb), whole(lam)],
        out_specs=[pl.BlockSpec((T_MIX, WIDTH), lambda s, i: (fwd(s, i), 0)),
                   pl.BlockSpec((T_MIX, WIDTH), lambda s, i: (bwd(s, i), 0))],
        out_shape=[jax.ShapeDtypeStruct((n_tok, WIDTH), O_DTYPE)] * 2,
        scratch_shapes=[pltpu.VMEM((WIDTH // LANES, SUBLANES * MIX_PITCH, LANES), F32),
                        pltpu.VMEM((T_MIX, WIDTH), F32),
                        pltpu.VMEM((T_MIX, WIDTH), F32),
                        pltpu.VMEM((2, SUBLANES, WIDTH), F32)],
        compiler_params=pltpu.CompilerParams(
            dimension_semantics=("arbitrary", "arbitrary"), vmem_limit_bytes=VMEM_LIMIT),
        name="mix_rglru",
    )(p, p, p, p, p, p, cw, cb, gw, gb, lam)


def _head_norm(o, w):
    parts = [_rms(o[:, h * HEAD_V:(h + 1) * HEAD_V], w) for h in range(WIDTH // HEAD_V)]
    return jnp.concatenate(parts, axis=1)


def _merge_kernel(x_ref, oaf_ref, oab_ref, oga_ref, obf_ref, obb_ref, ogb_ref, hcf_ref, hcb_ref, yc_ref,
                  ga_ref, gb_ref, gc_ref, nwa_ref, nwb_ref, wbr_ref, wout_ref, o_ref):
    ld = lambda ref: ref[...].astype(F32)
    o_a = _head_norm(ld(oaf_ref) + ld(oab_ref), nwa_ref[...]) * ld(oga_ref)
    o_b = _head_norm(ld(obf_ref) + ld(obb_ref), nwb_ref[...]) * ld(ogb_ref)
    o_c = (ld(hcf_ref) + ld(hcb_ref)) * ld(yc_ref)
    m = ld(ga_ref) * jnp.dot(o_a.astype(BF16), wbr_ref[0], preferred_element_type=F32)
    m = m + ld(gb_ref) * jnp.dot(o_b.astype(BF16), wbr_ref[1], preferred_element_type=F32)
    m = m + ld(gc_ref) * jnp.dot(o_c.astype(BF16), wbr_ref[2], preferred_element_type=F32)
    o_ref[...] = x_ref[...] + jnp.dot(m.astype(BF16), wout_ref[...], preferred_element_type=F32)


def _merge(x, p, oaf, oab, obf, obb, hcf, hcb, nwa, nwb, wbr, wout):
    n_tok = x.shape[0]
    tok = lambda width, offset: pl.BlockSpec((T_MERGE, width), lambda i: (i, offset // width))

    def whole(a):
        return pl.BlockSpec(a.shape, lambda i: (0,) * a.ndim, pipeline_mode=pl.Buffered(1))

    return pl.pallas_call(
        _merge_kernel,
        grid=(n_tok // T_MERGE,),
        in_specs=[tok(D_MODEL, 0), tok(WIDTH, 0), tok(WIDTH, 0), tok(WIDTH, COL_OGA),
                  tok(WIDTH, 0), tok(WIDTH, 0), tok(WIDTH, COL_OGB),
                  tok(WIDTH, 0), tok(WIDTH, 0), tok(WIDTH, COL_YC),
                  tok(D_MODEL, COL_GA), tok(D_MODEL, COL_GB), tok(D_MODEL, COL_GC),
                  whole(nwa), whole(nwb), whole(wbr), whole(wout)],
        out_specs=pl.BlockSpec((T_MERGE, D_MODEL), lambda i: (i, 0)),
        out_shape=jax.ShapeDtypeStruct((n_tok, D_MODEL), F32),
        compiler_params=pltpu.CompilerParams(
            dimension_semantics=("arbitrary",), vmem_limit_bytes=VMEM_LIMIT),
        name="merge",
    )(x, oaf, oab, p, obf, obb, p, hcf, hcb, p, p, p, p, nwa, nwb, wbr, wout)


def _ffn_kernel(x_ref, xp_ref, xn_ref, nw_ref, wup_ref, cw_ref, cb_ref, wd_ref, fw_ref, o_ref, h_ref,
                slab_ref, *, final_norm):
    i = pl.program_id(0)
    blocks_per_seq = SEQ // T_FFN
    first = (i % blocks_per_seq) == 0
    last = (i % blocks_per_seq) == blocks_per_seq - 1
    nw = nw_ref[...]

    n_slabs = D_MODEL // LANES

    for c in range(n_slabs):
        for s in range(SUBLANES):
            slab_ref[c, s * SLAB_PITCH:s * SLAB_PITCH + FFN_GROUPS, :] = x_ref[
                s * FFN_GROUPS:(s + 1) * FFN_GROUPS, c * LANES:(c + 1) * LANES]

    def gather(start, stride):
        return jnp.concatenate([slab_ref[c, pl.ds(start, SUBLANES, stride=stride), :]
                                for c in range(n_slabs)], axis=1)

    for jj in range(FFN_GROUPS // 2):
        xb = jnp.concatenate([gather(2 * jj, SLAB_PITCH), gather(2 * jj + 1, SLAB_PITCH)], axis=0)
        h_ref[2 * SUBLANES * jj:2 * SUBLANES * (jj + 1), :] = _rms(xb, nw).astype(BF16)
    halo = jnp.concatenate([jnp.where(first, 0.0, _rms(xp_ref[...], nw)),
                            jnp.where(last, 0.0, _rms(xn_ref[...], nw))], axis=0)
    h_ref[T_FFN:, :] = halo.astype(BF16)

    def up(f):
        h = h_ref[...]
        return tuple(jnp.dot(h, wup_ref[:, c * D_FF + f * F_TILE:c * D_FF + (f + 1) * F_TILE],
                             preferred_element_type=F32) for c in range(2))

    def conv(u, c, f):
        cols = slice(c * D_FF + f * F_TILE, c * D_FF + (f + 1) * F_TILE)
        body = u[:T_FFN]
        before = u[T_FFN + SUBLANES - 1:T_FFN + SUBLANES]
        after = u[T_FFN + SUBLANES:T_FFN + SUBLANES + 1]
        prev0 = jnp.concatenate([before, body[T_FFN - SUBLANES:T_FFN - 1]], axis=0)
        next_last = jnp.concatenate([body[1:SUBLANES], after], axis=0)
        u_prev = jnp.concatenate([prev0, body[:T_FFN - SUBLANES]], axis=0)
        u_next = jnp.concatenate([body[SUBLANES:], next_last], axis=0)
        return (cb_ref[:, cols] + u_prev * cw_ref[0:1, cols] + body * cw_ref[1:2, cols]
                + u_next * cw_ref[2:3, cols])

    n_f = D_FF // F_TILE
    acc = None
    pending = [up(f) for f in range(FFN_UP_AHEAD)]
    for f in range(n_f):
        ug, uv = pending.pop(0)
        if f + FFN_UP_AHEAD < n_f:
            pending.append(up(f + FFN_UP_AHEAD))
        act = _gelu_tanh(conv(ug, 0, f)) * conv(uv, 1, f)
        down = jnp.dot(act.astype(BF16), wd_ref[f * F_TILE:(f + 1) * F_TILE, :],
                       preferred_element_type=F32)
        acc = down if acc is None else acc + down

    for c in range(n_slabs):
        slab_ref[c, 0:T_FFN, :] = acc[:, c * LANES:(c + 1) * LANES]
    groups_per_sublane = FFN_GROUPS // SUBLANES
    for k in range(FFN_GROUPS):
        rows = slice(k * SUBLANES, (k + 1) * SUBLANES)
        s, j0 = k // groups_per_sublane, (k % groups_per_sublane) * SUBLANES
        y = x_ref[rows, :] + gather(j0 * SUBLANES + s, SUBLANES)
        o_ref[rows, :] = _rms(y, fw_ref[...]) if final_norm else y


def _ffn(x, nw, w_up, cw, cb, w_down, fw, final_norm):
    n_tok = x.shape[0]
    halo = T_FFN // SUBLANES
    last_halo = n_tok // SUBLANES - 1

    def resident(a):
        return pl.BlockSpec(a.shape, lambda i: (0,) * a.ndim, pipeline_mode=pl.Buffered(1))

    return pl.pallas_call(
        functools.partial(_ffn_kernel, final_norm=final_norm),
        grid=(n_tok // T_FFN,),
        in_specs=[
            pl.BlockSpec((T_FFN, D_MODEL), lambda i: (i, 0)),
            pl.BlockSpec((SUBLANES, D_MODEL), lambda i: (jnp.maximum(i * halo - 1, 0), 0)),
            pl.BlockSpec((SUBLANES, D_MODEL), lambda i: (jnp.minimum((i + 1) * halo, last_halo), 0)),
            resident(nw), resident(w_up), resident(cw), resident(cb), resident(w_down), resident(fw),
        ],
        out_specs=pl.BlockSpec((T_FFN, D_MODEL), lambda i: (i, 0)),
        out_shape=jax.ShapeDtypeStruct((n_tok, D_MODEL), F32),
        scratch_shapes=[pltpu.VMEM((T_FFN + 2 * SUBLANES, D_MODEL), BF16),
                        pltpu.VMEM((D_MODEL // LANES, SUBLANES * SLAB_PITCH, LANES), F32)],
        compiler_params=pltpu.CompilerParams(
            dimension_semantics=("arbitrary",), vmem_limit_bytes=VMEM_LIMIT),
        name="ffn",
    )(x, x, x, nw, w_up, cw, cb, w_down, fw)


def _layout_w_in(w_in_l):
    (q_a, zf_a, zb_a, i_a, og_a, q_b, k_b, v_b, og_b, lrf, lrb, x_c, y_c, g_a, g_b, g_c) = jnp.split(
        w_in_l.astype(BF16),
        [512, 1024, 1536, 2048, 2560, 2816, 3072, 3584, 4096, 4112, 4128, 4640, 5152, 6176, 7200], axis=1)
    pad = jnp.zeros((D_MODEL, LANES - 2 * GLA_RANK), BF16)
    q_b = q_b * GLA_Q_SCALE
    return jnp.concatenate([q_a, zf_a, zb_a, i_a, og_a, q_b, k_b, v_b, og_b, x_c, y_c, g_a, g_b, g_c,
                            lrf, lrb, pad], axis=1)


def _layout_gla_up(up_w_l):
    out = jnp.zeros((2, LANES, B_KEY), F32)
    out = out.at[0, 0:GLA_RANK].set(up_w_l[0])
    out = out.at[1, GLA_RANK:2 * GLA_RANK].set(up_w_l[1])
    return out.astype(BF16)


def _layout_rglru_gates(wa_l, wx_l):
    def tiles(w):
        w = w.reshape(2, 2, 4, C_BLOCK, C_BLOCK)
        eye = jnp.eye(4, dtype=w.dtype)
        t = jnp.einsum('dhbkj,bc->dhbkcj', w, eye)
        return t.reshape(2, 2, 4 * C_BLOCK, 4 * C_BLOCK)
    return jnp.stack([tiles(wa_l), tiles(wx_l)], axis=1).astype(BF16)


def kernel(x_prompt, x_sample, norm_mix_w, w_in, hgrn_lb_logits, hgrn_norm_w, gla_up_w, gla_up_b,
           gla_norm_w, c_conv_w, c_conv_b, rglru_wa, rglru_ba, rglru_wx, rglru_bx, rglru_lam,
           w_branch, w_out, norm_ffn_w, ffn_up, ffn_conv_w, ffn_conv_b, ffn_down, final_norm_w):
    lb_all = jnp.cumsum(jax.nn.softmax(hgrn_lb_logits.astype(F32), axis=0), axis=0)
    lb_all = lb_all - lb_all[0]
    fw = final_norm_w.reshape(1, D_MODEL)

    xs = [x_prompt.reshape(-1, D_MODEL), x_sample.reshape(-1, D_MODEL)]
    for l in range(DEPTH):
        w_in_l = _layout_w_in(w_in[l])
        up_w_l = _layout_gla_up(gla_up_w[l])
        gate_w_l = _layout_rglru_gates(rglru_wa[l], rglru_wx[l])
        gate_b_l = jnp.stack([rglru_ba[l], rglru_bx[l]], axis=1)
        w_branch_l, w_out_l = w_branch[l].astype(BF16), w_out[l].astype(BF16)
        ffn_up_l, ffn_down_l = ffn_up[l].astype(BF16), ffn_down[l].astype(BF16)
        for n, x in enumerate(xs):
            p = _proj(x, norm_mix_w[l].reshape(1, D_MODEL), lb_all[l], w_in_l)
            oaf, oab, obf, obb = _bidir_gla(p, up_w_l, gla_up_b[l].reshape(2, 1, B_KEY))
            hcf, hcb = _rglru(p, c_conv_w[l], c_conv_b[l].reshape(1, WIDTH), gate_w_l, gate_b_l,
                              rglru_lam[l])
            x = _merge(x, p, oaf, oab, obf, obb, hcf, hcb,
                       hgrn_norm_w[l].reshape(1, HEAD_V), gla_norm_w[l].reshape(1, HEAD_V),
                       w_branch_l, w_out_l)
            xs[n] = _ffn(x, norm_ffn_w[l].reshape(1, D_MODEL), ffn_up_l, ffn_conv_w[l],
                         ffn_conv_b[l].reshape(1, 2 * D_FF), ffn_down_l, fw,
                         final_norm=(l == DEPTH - 1))

    return (xs[0].reshape(x_prompt.shape), xs[1].reshape(x_sample.shape))
```

```python
import functools
import math

import jax
import jax.numpy as jnp
from jax import lax
from jax.experimental import pallas as pl
from jax.experimental.pallas import tpu as pltpu

F32 = jnp.float32
BF16 = jnp.bfloat16
P_DTYPE = BF16
O_DTYPE = BF16

D_MODEL = 1024
SEQ = 8192
DEPTH = 4
WIDTH = 512
HEAD_V = 128
A_KEY = 512
B_KEY = 256
GLA_RANK = 16
GLA_NORMALIZER = 16.0
GLA_PAIR = 2
GLA_Q_SCALE = 0.125
C_BLOCK = 64
C_CONV = 4
RG_C = 8.0
D_FF = 2816
FFN_CONV = 3
EPS = 1e-6

LANES = 128
SUBLANES = 8
HALO = 16
VMEM_LIMIT = 56 * 1024 * 1024

N_PROJ = 8320
COL_QA, COL_ZF, COL_ZB, COL_IA, COL_OGA = 0, 512, 1024, 1536, 2048
COL_QB, COL_KB, COL_VB, COL_OGB = 2560, 2816, 3072, 3584
COL_XC, COL_YC = 4096, 4608
COL_GA, COL_GB, COL_GC = 5120, 6144, 7168
COL_LR = 8192

SUB = 16
CHUNK = 64
N_SUB = CHUNK // SUB
PHASE_LAG = 4
T_MIX = 1024
MIX_GROUPS = T_MIX // SUBLANES
MIX_PITCH = MIX_GROUPS + SUBLANES
T_PROJ = 512
N_PROJ_TILE = 2048
T_MERGE = 512
T_FFN = 512
FFN_UP_AHEAD = 11
FFN_GROUPS = T_FFN // SUBLANES
SLAB_PITCH = FFN_GROUPS + SUBLANES
F_TILE = 256


def _sigmoid(x):
    return 1.0 / (1.0 + jnp.exp(-x))


def _sigmoid_tanh(x):
    return 0.5 + 0.5 * jnp.tanh(0.5 * x)


def _gelu_tanh(x):
    return 0.5 * x * (1.0 + jnp.tanh(math.sqrt(2.0 / math.pi) * (x + 0.044715 * (x * x * x))))


def _rms(x, w):
    ms = jnp.mean(x * x, axis=-1, keepdims=True)
    return x * lax.rsqrt(ms + EPS) * w


def _proj_activation(col, u, lb_ref):
    if col == COL_QA:
        return u * _sigmoid_tanh(u) * (HEAD_V ** -0.5)
    if col in (COL_ZF, COL_ZB):
        lb = lb_ref[(col - COL_ZF) // WIDTH:(col - COL_ZF) // WIDTH + 1, :]
        return jnp.log(lb + (1.0 - lb) * _sigmoid(u))
    if col in (COL_OGA, COL_OGB):
        return u * _sigmoid_tanh(u)
    if col == COL_YC:
        return _gelu_tanh(u)
    if COL_GA <= col < COL_LR:
        return _sigmoid_tanh(u)
    return u


def _proj_kernel(x_ref, nw_ref, lb_ref, w_ref, o_ref):
    h = _rms(x_ref[...], nw_ref[...]).astype(BF16)
    for lo in range(0, N_PROJ, N_PROJ_TILE):
        hi = min(lo + N_PROJ_TILE, N_PROJ)
        u = jnp.dot(h, w_ref[:, lo:hi], preferred_element_type=F32)
        for col in range(lo, hi, WIDTH):
            end = min(col + WIDTH, hi)
            o_ref[:, col:end] = _proj_activation(col, u[:, col - lo:end - lo], lb_ref).astype(o_ref.dtype)


def _proj(x, nw, lb, w):
    n_tok = x.shape[0]
    return pl.pallas_call(
        _proj_kernel,
        grid=(n_tok // T_PROJ,),
        in_specs=[
            pl.BlockSpec((T_PROJ, D_MODEL), lambda i: (i, 0)),
            pl.BlockSpec((1, D_MODEL), lambda i: (0, 0), pipeline_mode=pl.Buffered(1)),
            pl.BlockSpec((2, WIDTH), lambda i: (0, 0), pipeline_mode=pl.Buffered(1)),
            pl.BlockSpec((D_MODEL, N_PROJ), lambda i: (0, 0), pipeline_mode=pl.Buffered(1)),
        ],
        out_specs=pl.BlockSpec((T_PROJ, N_PROJ), lambda i: (i, 0)),
        out_shape=jax.ShapeDtypeStruct((n_tok, N_PROJ), P_DTYPE),
        compiler_params=pltpu.CompilerParams(
            dimension_semantics=("arbitrary",), vmem_limit_bytes=VMEM_LIMIT),
        name="proj",
    )(x, nw, lb, w)


def _tri_consts(reverse):
    i = lax.broadcasted_iota(jnp.int32, (CHUNK, CHUNK), 0)
    j = lax.broadcasted_iota(jnp.int32, (CHUNK, CHUNK), 1)
    causal = (j >= i) if reverse else (j <= i)
    same_sub = (i // SUB) == (j // SUB)
    return jnp.where(causal & same_sub, 1.0, 0.0).astype(BF16), causal


def _chunk_cumsum(lg, cum_mat):
    lg_hi = lg.astype(BF16)
    lg_lo = (lg - lg_hi.astype(F32)).astype(BF16)
    return jnp.dot(cum_mat, jnp.concatenate([lg_hi, lg_lo], axis=1), preferred_element_type=F32)


def _chunk_scores(cum, q, k, v_list, st_list, head_masks, reverse):
    n_heads = len(v_list)
    b_loc = cum[:, :LANES] + cum[:, LANES:]
    order = range(N_SUB - 1, -1, -1) if reverse else range(N_SUB)
    offsets, total = [None] * N_SUB, jnp.zeros((1, LANES), F32)
    for s in order:
        offsets[s] = total
        edge = s * SUB if reverse else (s + 1) * SUB - 1
        total = total + b_loc[edge:edge + 1, :]
    r = jnp.concatenate([jnp.broadcast_to(o, (SUB, LANES)) for o in offsets], axis=0)
    b = b_loc + r
    q_loc = q * jnp.exp(b_loc)
    k_inv = k * jnp.exp(-b_loc)
    q_dec = (q_loc * jnp.exp(r)).astype(BF16)
    last = 0 if reverse else CHUNK - 1
    b_last = b[last:last + 1, :]
    k_tail = (k * jnp.exp(b_last - b)).astype(BF16)
    e_last = jnp.exp(b_last)

    q_loc_bf = q_loc.astype(BF16)
    k_groups = []
    q_groups = []
    for s in range(N_SUB):
        lo, hi = (s * SUB, CHUNK) if reverse else (0, (s + 1) * SUB)
        r_s = r[s * SUB:s * SUB + 1, :]
        kt = (k_inv[lo:hi] * jnp.exp(r_s - r[lo:hi])).astype(BF16)
        pieces = []
        if lo > 0:
            pieces.append(jnp.zeros((lo, LANES), BF16))
        pieces.append(kt)
        if hi < CHUNK:
            pieces.append(jnp.zeros((CHUNK - hi, LANES), BF16))
        k_groups.append(jnp.concatenate(pieces, axis=0) if len(pieces) > 1 else kt)
        qp = []
        if s > 0:
            qp.append(jnp.zeros((s * SUB, LANES), BF16))
        qp.append(q_loc_bf[s * SUB:(s + 1) * SUB])
        if s < N_SUB - 1:
            qp.append(jnp.zeros((CHUNK - (s + 1) * SUB, LANES), BF16))
        q_groups.append(jnp.concatenate(qp, axis=0))
    k_wide = jnp.concatenate(k_groups, axis=1)
    q_wide = jnp.concatenate(q_groups, axis=1)
    if head_masks[0] is None:
        q_stack = q_wide
    else:
        zero = jnp.zeros_like(q_wide)
        q_stack = jnp.concatenate([jnp.where(m[0], q_wide, zero) for m in head_masks], axis=0)
    att = lax.dot_general(q_stack, k_wide, (((1,), (1,)), ((), ())), preferred_element_type=F32)

    o_inter, upd = [], []
    for h in range(n_heads):
        qd = q_dec if head_masks[h] is None else jnp.where(head_masks[h][1], q_dec, jnp.zeros_like(q_dec))
        o_inter.append(lax.dot_general(qd, st_list[h].astype(BF16), (((1,), (1,)), ((), ())),
                                       preferred_element_type=F32))
        upd.append(lax.dot_general(v_list[h], k_tail, (((0,), (0,)), ((), ())),
                                   preferred_element_type=F32))
    return att, o_inter, upd, e_last


def _chunk_outputs(att, o_inter, upd, e_last, v_list, st_list, causal):
    outs, new_states = [], []
    for h in range(len(v_list)):
        att_h = jnp.where(causal, att[h * CHUNK:(h + 1) * CHUNK], 0.0).astype(BF16)
        outs.append(jnp.dot(att_h, v_list[h], preferred_element_type=F32) + o_inter[h])
        new_states.append(st_list[h] * e_last + upd[h])
    return outs, new_states


def _gla_kernel(aqf_ref, azf_ref, avf_ref, aqb_ref, azb_ref, avb_ref,
                bqf_ref, bkf_ref, bvf_ref, blrf_ref, bqb_ref, bkb_ref, bvb_ref, blrb_ref, upw_ref, upb_ref,
                oaf_ref, oab_ref, obf_ref, obb_ref, sta_ref, stb_ref, lg_ref):
    n_heads = WIDTH // HEAD_V

    @pl.when(pl.program_id(1) == 0)
    def _():
        sta_ref[...] = jnp.zeros_like(sta_ref)
        stb_ref[...] = jnp.zeros_like(stb_ref)

    for d, lr_ref in enumerate((blrf_ref, blrb_ref)):
        pre = jnp.dot(lr_ref[...].astype(BF16), upw_ref[d], preferred_element_type=F32) + upb_ref[d]
        log_sig = jnp.minimum(pre, 0.0) - jnp.log(1.0 + jnp.exp(-jnp.abs(pre)))
        lg_ref[d] = log_sig / GLA_NORMALIZER

    consts = [_tri_consts(False), _tri_consts(True)]
    width = LANES // GLA_PAIR
    lane_w = lax.broadcasted_iota(jnp.int32, (CHUNK, N_SUB * LANES), 1) % LANES
    lane = lax.broadcasted_iota(jnp.int32, (CHUNK, LANES), 1)
    pair_masks = [tuple((ln >= h * width) & (ln < (h + 1) * width) for ln in (lane_w, lane))
                  for h in range(GLA_PAIR)]
    n_chunks = T_MIX // CHUNK

    hgrn = [("a", d, g) for g in range(A_KEY // LANES) for d in range(2)]
    gla = [("b", d, g) for g in range(B_KEY // LANES) for d in range(2)]
    chains = []
    while hgrn or gla:
        chains += hgrn[:2] + gla[:1]
        hgrn, gla = hgrn[2:], gla[1:]

    def body(c, carry):
        work = [dict() for _ in chains]

        def phase1(w, mixer, d, g):
            reverse = d == 1
            cc = (n_chunks - 1 - c) if reverse else c
            w["rows"] = rows = pl.ds(pl.multiple_of(cc * CHUNK, CHUNK), CHUNK)
            cols = slice(g * LANES, (g + 1) * LANES)
            if mixer == "a":
                w["q"] = (aqb_ref if reverse else aqf_ref)[rows, cols].astype(F32)
                lg = (azb_ref if reverse else azf_ref)[rows, cols].astype(F32)
                w["k"] = 1.0 - jnp.exp(lg)
            else:
                w["q"] = (bqb_ref if reverse else bqf_ref)[rows, cols].astype(F32)
                w["k"] = (bkb_ref if reverse else bkf_ref)[rows, cols].astype(F32)
                lg = lg_ref[d, rows, cols]
            w["cum"] = _chunk_cumsum(lg, consts[d][0])

        def phase2(w, mixer, d, g):
            if mixer == "a":
                v_ref, st_ref, hs, masks = (avb_ref if d == 1 else avf_ref), sta_ref, [g], [None]
            else:
                v_ref, st_ref = (bvb_ref if d == 1 else bvf_ref), stb_ref
                hs, masks = [g * GLA_PAIR + h for h in range(GLA_PAIR)], pair_masks
            w["hs"] = hs
            w["v"] = [v_ref[w["rows"], h * HEAD_V:(h + 1) * HEAD_V].astype(BF16) for h in hs]
            w["st"] = [st_ref[d * n_heads + h] for h in hs]
            w["scores"] = _chunk_scores(w["cum"], w["q"], w["k"], w["v"], w["st"], masks, d == 1)

        def phase3(w, mixer, d, g):
            if mixer == "a":
                o_ref, st_ref = (oab_ref if d == 1 else oaf_ref), sta_ref
            else:
                o_ref, st_ref = (obb_ref if d == 1 else obf_ref), stb_ref
            outs, new_states = _chunk_outputs(*w["scores"], w["v"], w["st"], consts[d][1])
            for h, o, st in zip(w["hs"], outs, new_states):
                o_ref[w["rows"], h * HEAD_V:(h + 1) * HEAD_V] = o.astype(o_ref.dtype)
                st_ref[d * n_heads + h] = st

        phases = (phase1, phase2, phase3)
        for slot in range(len(chains) + PHASE_LAG * (len(phases) - 1)):
            for p, phase in enumerate(phases):
                n = slot - p * PHASE_LAG
                if 0 <= n < len(chains):
                    phase(work[n], *chains[n])
        return carry

    lax.fori_loop(0, n_chunks, body, 0, unroll=8)


def _bidir_gla(p, upw, upb):
    n_tok = p.shape[0]
    n_seq = n_tok // SEQ
    nb = SEQ // T_MIX
    fwd = lambda s, i: s * nb + i
    bwd = lambda s, i: s * nb + (nb - 1 - i)

    def col(width, offset, blk):
        return pl.BlockSpec((T_MIX, width), lambda s, i: (blk(s, i), offset // width))

    def whole(a):
        return pl.BlockSpec(a.shape, lambda s, i: (0,) * a.ndim)

    in_specs = [col(WIDTH, COL_QA, fwd), col(WIDTH, COL_ZF, fwd), col(WIDTH, COL_IA, fwd),
                col(WIDTH, COL_QA, bwd), col(WIDTH, COL_ZB, bwd), col(WIDTH, COL_IA, bwd),
                col(B_KEY, COL_QB, fwd), col(B_KEY, COL_KB, fwd), col(WIDTH, COL_VB, fwd),
                col(LANES, COL_LR, fwd),
                col(B_KEY, COL_QB, bwd), col(B_KEY, COL_KB, bwd), col(WIDTH, COL_VB, bwd),
                col(LANES, COL_LR, bwd), whole(upw), whole(upb)]
    out_spec_f = pl.BlockSpec((T_MIX, WIDTH), lambda s, i: (fwd(s, i), 0))
    out_spec_b = pl.BlockSpec((T_MIX, WIDTH), lambda s, i: (bwd(s, i), 0))
    n_states = 2 * WIDTH // HEAD_V
    return pl.pallas_call(
        _gla_kernel,
        grid=(n_seq, nb),
        in_specs=in_specs,
        out_specs=[out_spec_f, out_spec_b, out_spec_f, out_spec_b],
        out_shape=[jax.ShapeDtypeStruct((n_tok, WIDTH), O_DTYPE)] * 4,
        scratch_shapes=[pltpu.VMEM((n_states, HEAD_V, LANES), F32), pltpu.VMEM((n_states, HEAD_V, LANES), F32),
                        pltpu.VMEM((2, T_MIX, B_KEY), F32)],
        compiler_params=pltpu.CompilerParams(
            dimension_semantics=("arbitrary", "arbitrary"), vmem_limit_bytes=VMEM_LIMIT),
        name="mix_gla",
    )(*([p] * 14), upw, upb)


def _rglru_kernel(xf_ref, xfp_ref, xfn_ref, xb_ref, xbp_ref, xbn_ref, cw_ref, cb_ref, gw_ref, gb_ref,
                  lam_ref, of_ref, ob_ref, slab_ref, a_ref, u_ref, carry_ref):
    i = pl.program_id(1)
    nb = pl.num_programs(1)

    @pl.when(i == 0)
    def _():
        carry_ref[...] = jnp.zeros_like(carry_ref)

    n_slabs = WIDTH // LANES
    top = T_MIX - SUBLANES

    def gather(start, stride):
        return jnp.concatenate([slab_ref[c, pl.ds(start, SUBLANES, stride=stride), :]
                                for c in range(n_slabs)], axis=1)

    for d in range(2):
        reverse = d == 1
        x_ref, xp_ref, xn_ref, o_ref = (xb_ref, xbp_ref, xbn_ref, ob_ref) if reverse else (
            xf_ref, xfp_ref, xfn_ref, of_ref)
        blk = (nb - 1 - i) if reverse else i
        xf = x_ref[...].astype(F32)
        for c in range(n_slabs):
            for s in range(SUBLANES):
                slab_ref[c, s * MIX_PITCH:s * MIX_PITCH + MIX_GROUPS, :] = xf[
                    s * MIX_GROUPS:(s + 1) * MIX_GROUPS, c * LANES:(c + 1) * LANES]
        groups = [gather(j, MIX_PITCH) for j in range(MIX_GROUPS)]
        x = jnp.concatenate(groups, axis=0)
        before = jnp.where(blk == 0, 0.0, xp_ref[...].astype(F32))
        after = jnp.where(blk == nb - 1, 0.0, xn_ref[...].astype(F32))
        down1 = jnp.concatenate([before[HALO - 1:HALO], groups[-1][:SUBLANES - 1]], axis=0)
        down2 = jnp.concatenate([before[HALO - 2:HALO - 1], groups[-2][:SUBLANES - 1]], axis=0)
        up1 = jnp.concatenate([groups[0][1:], after[0:1]], axis=0)
        x_m1 = jnp.concatenate([down1, x[:top]], axis=0)
        x_m2 = jnp.concatenate([down2, down1, x[:top - SUBLANES]], axis=0)
        x_p1 = jnp.concatenate([x[SUBLANES:], up1], axis=0)
        xc = (cb_ref[...] + x_m2 * cw_ref[0:1, :] + x_m1 * cw_ref[1:2, :] + x * cw_ref[2:3, :]
              + x_p1 * cw_ref[3:4, :])
        xc_bf = xc.astype(BF16)
        gates = []
        for g in range(2):
            halves = [jnp.dot(xc_bf[:, hh * 256:(hh + 1) * 256], gw_ref[d, g, hh],
                              preferred_element_type=F32) for hh in range(2)]
            gates.append(_sigmoid_tanh(jnp.concatenate(halves, axis=1) + gb_ref[d, g:g + 1, :]))
        r_gate, i_gate = gates
        lam = lam_ref[d:d + 1, :]
        softplus_neg_lam = jnp.maximum(-lam, 0.0) + jnp.log(1.0 + jnp.exp(-jnp.abs(lam)))
        log_a = -RG_C * r_gate * softplus_neg_lam
        a = jnp.exp(log_a)
        one_minus_a2 = -jnp.tanh(log_a) * (1.0 + a * a)
        root = jnp.where(one_minus_a2 > 0.0, one_minus_a2 * lax.rsqrt(one_minus_a2), 0.0)
        u = root * (i_gate * xc)
        a_ref[...] = a
        u_ref[...] = u

        def body(n, state):
            h_prev, p_prev = state
            j = (MIX_GROUPS - 1 - n) if reverse else n
            rows = pl.ds(pl.multiple_of(j * SUBLANES, SUBLANES), SUBLANES)
            a_j = a_ref[rows, :]
            h = a_j * h_prev + u_ref[rows, :]
            p = a_j * p_prev
            u_ref[rows, :] = h
            a_ref[rows, :] = p
            return h, p

        init = (jnp.zeros((SUBLANES, WIDTH), F32), jnp.ones((SUBLANES, WIDTH), F32))
        h_end, p_end = lax.fori_loop(0, MIX_GROUPS, body, init, unroll=8)
        carry = carry_ref[d][0:1, :]
        carries = [None] * SUBLANES
        for s in (range(SUBLANES - 1, -1, -1) if reverse else range(SUBLANES)):
            carries[s] = carry
            carry = h_end[s:s + 1, :] + p_end[s:s + 1, :] * carry
        carry_ref[d] = jnp.broadcast_to(carry, (SUBLANES, WIDTH))
        carry_in = jnp.concatenate(carries, axis=0)
        h_all = u_ref[...] + a_ref[...] * jnp.concatenate([carry_in] * MIX_GROUPS, axis=0)
        for c in range(n_slabs):
            slab_ref[c, 0:T_MIX, :] = h_all[:, c * LANES:(c + 1) * LANES]
        groups_per_sublane = MIX_GROUPS // SUBLANES
        for k2 in range(MIX_GROUPS // 2):
            pair = []
            for k in (2 * k2, 2 * k2 + 1):
                s, j0 = k // groups_per_sublane, (k % groups_per_sublane) * SUBLANES
                pair.append(gather(j0 * SUBLANES + s, SUBLANES))
            o_ref[2 * SUBLANES * k2:2 * SUBLANES * (k2 + 1), :] = jnp.concatenate(pair, axis=0).astype(
                o_ref.dtype)


def _rglru(p, cw, cb, gw, gb, lam):
    n_tok = p.shape[0]
    n_seq = n_tok // SEQ
    nb = SEQ // T_MIX
    halo = T_MIX // HALO
    last_halo = n_tok // HALO - 1
    xcol = COL_XC // WIDTH
    fwd = lambda s, i: s * nb + i
    bwd = lambda s, i: s * nb + (nb - 1 - i)

    def specs(blk):
        return [
            pl.BlockSpec((T_MIX, WIDTH), lambda s, i: (blk(s, i), xcol)),
            pl.BlockSpec((HALO, WIDTH), lambda s, i: (jnp.maximum(blk(s, i) * halo - 1, 0), xcol)),
            pl.BlockSpec((HALO, WIDTH),
                         lambda s, i: (jnp.minimum((blk(s, i) + 1) * halo, last_halo), xcol)),
        ]

    def whole(a):
        return pl.BlockSpec(a.shape, lambda s, i: (0,) * a.ndim)

    return pl.pallas_call(
        _rglru_kernel,
        grid=(n_seq, nb),
        in_specs=specs(fwd) + specs(bwd) + [whole(cw), whole(cb), whole(gw), whole(gb), whole(lam)],
        out_specs=[pl.BlockSpec((T_MIX, WIDTH), lambda s, i: (fwd(s, i), 0)),
                   pl.BlockSpec((T_MIX, WIDTH), lambda s, i: (bwd(s, i), 0))],
        out_shape=[jax.ShapeDtypeStruct((n_tok, WIDTH), O_DTYPE)] * 2,
        scratch_shapes=[pltpu.VMEM((WIDTH // LANES, SUBLANES * MIX_PITCH, LANES), F32),
                        pltpu.VMEM((T_MIX, WIDTH), F32),
                        pltpu.VMEM((T_MIX, WIDTH), F32),
                        pltpu.VMEM((2, SUBLANES, WIDTH), F32)],
        compiler_params=pltpu.CompilerParams(
            dimension_semantics=("arbitrary", "arbitrary"), vmem_limit_bytes=VMEM_LIMIT),
        name="mix_rglru",
    )(p, p, p, p, p, p, cw, cb, gw, gb, lam)


def _head_norm(o, w):
    parts = [_rms(o[:, h * HEAD_V:(h + 1) * HEAD_V], w) for h in range(WIDTH // HEAD_V)]
    return jnp.concatenate(parts, axis=1)


def _merge_kernel(x_ref, oaf_ref, oab_ref, oga_ref, obf_ref, obb_ref, ogb_ref, hcf_ref, hcb_ref, yc_ref,
                  ga_ref, gb_ref, gc_ref, nwa_ref, nwb_ref, wbr_ref, wout_ref, o_ref):
    ld = lambda ref: ref[...].astype(F32)
    o_a = _head_norm(ld(oaf_ref) + ld(oab_ref), nwa_ref[...]) * ld(oga_ref)
    o_b = _head_norm(ld(obf_ref) + ld(obb_ref), nwb_ref[...]) * ld(ogb_ref)
    o_c = (ld(hcf_ref) + ld(hcb_ref)) * ld(yc_ref)
    m = ld(ga_ref) * jnp.dot(o_a.astype(BF16), wbr_ref[0], preferred_element_type=F32)
    m = m + ld(gb_ref) * jnp.dot(o_b.astype(BF16), wbr_ref[1], preferred_element_type=F32)
    m = m + ld(gc_ref) * jnp.dot(o_c.astype(BF16), wbr_ref[2], preferred_element_type=F32)
    o_ref[...] = x_ref[...] + jnp.dot(m.astype(BF16), wout_ref[...], preferred_element_type=F32)


def _merge(x, p, oaf, oab, obf, obb, hcf, hcb, nwa, nwb, wbr, wout):
    n_tok = x.shape[0]
    tok = lambda width, offset: pl.BlockSpec((T_MERGE, width), lambda i: (i, offset // width))

    def whole(a):
        return pl.BlockSpec(a.shape, lambda i: (0,) * a.ndim, pipeline_mode=pl.Buffered(1))

    return pl.pallas_call(
        _merge_kernel,
        grid=(n_tok // T_MERGE,),
        in_specs=[tok(D_MODEL, 0), tok(WIDTH, 0), tok(WIDTH, 0), tok(WIDTH, COL_OGA),
                  tok(WIDTH, 0), tok(WIDTH, 0), tok(WIDTH, COL_OGB),
                  tok(WIDTH, 0), tok(WIDTH, 0), tok(WIDTH, COL_YC),
                  tok(D_MODEL, COL_GA), tok(D_MODEL, COL_GB), tok(D_MODEL, COL_GC),
                  whole(nwa), whole(nwb), whole(wbr), whole(wout)],
        out_specs=pl.BlockSpec((T_MERGE, D_MODEL), lambda i: (i, 0)),
        out_shape=jax.ShapeDtypeStruct((n_tok, D_MODEL), F32),
        compiler_params=pltpu.CompilerParams(
            dimension_semantics=("arbitrary",), vmem_limit_bytes=VMEM_LIMIT),
        name="merge",
    )(x, oaf, oab, p, obf, obb, p, hcf, hcb, p, p, p, p, nwa, nwb, wbr, wout)


def _ffn_kernel(x_ref, xp_ref, xn_ref, nw_ref, wup_ref, cw_ref, cb_ref, wd_ref, fw_ref, o_ref, h_ref,
                slab_ref, *, final_norm):
    i = pl.program_id(0)
    blocks_per_seq = SEQ // T_FFN
    first = (i % blocks_per_seq) == 0
    last = (i % blocks_per_seq) == blocks_per_seq - 1
    nw = nw_ref[...]

    n_slabs = D_MODEL // LANES

    for c in range(n_slabs):
        for s in range(SUBLANES):
            slab_ref[c, s * SLAB_PITCH:s * SLAB_PITCH + FFN_GROUPS, :] = x_ref[
                s * FFN_GROUPS:(s + 1) * FFN_GROUPS, c * LANES:(c + 1) * LANES]

    def gather(start, stride):
        return jnp.concatenate([slab_ref[c, pl.ds(start, SUBLANES, stride=stride), :]
                                for c in range(n_slabs)], axis=1)

    for jj in range(FFN_GROUPS // 2):
        xb = jnp.concatenate([gather(2 * jj, SLAB_PITCH), gather(2 * jj + 1, SLAB_PITCH)], axis=0)
        h_ref[2 * SUBLANES * jj:2 * SUBLANES * (jj + 1), :] = _rms(xb, nw).astype(BF16)
    halo = jnp.concatenate([jnp.where(first, 0.0, _rms(xp_ref[...], nw)),
                            jnp.where(last, 0.0, _rms(xn_ref[...], nw))], axis=0)
    h_ref[T_FFN:, :] = halo.astype(BF16)

    def up(f):
        h = h_ref[...]
        return tuple(jnp.dot(h, wup_ref[:, c * D_FF + f * F_TILE:c * D_FF + (f + 1) * F_TILE],
                             preferred_element_type=F32) for c in range(2))

    def conv(u, c, f):
        cols = slice(c * D_FF + f * F_TILE, c * D_FF + (f + 1) * F_TILE)
        body = u[:T_FFN]
        before = u[T_FFN + SUBLANES - 1:T_FFN + SUBLANES]
        after = u[T_FFN + SUBLANES:T_FFN + SUBLANES + 1]
        prev0 = jnp.concatenate([before, body[T_FFN - SUBLANES:T_FFN - 1]], axis=0)
        next_last = jnp.concatenate([body[1:SUBLANES], after], axis=0)
        u_prev = jnp.concatenate([prev0, body[:T_FFN - SUBLANES]], axis=0)
        u_next = jnp.concatenate([body[SUBLANES:], next_last], axis=0)
        return (cb_ref[:, cols] + u_prev * cw_ref[0:1, cols] + body * cw_ref[1:2, cols]
                + u_next * cw_ref[2:3, cols])

    n_f = D_FF // F_TILE
    acc = None
    pending = [up(f) for f in range(FFN_UP_AHEAD)]
    for f in range(n_f):
        ug, uv = pending.pop(0)
        if f + FFN_UP_AHEAD < n_f:
            pending.append(up(f + FFN_UP_AHEAD))
        act = _gelu_tanh(conv(ug, 0, f)) * conv(uv, 1, f)
        down = jnp.dot(act.astype(BF16), wd_ref[f * F_TILE:(f + 1) * F_TILE, :],
                       preferred_element_type=F32)
        acc = down if acc is None else acc + down

    for c in range(n_slabs):
        slab_ref[c, 0:T_FFN, :] = acc[:, c * LANES:(c + 1) * LANES]
    groups_per_sublane = FFN_GROUPS // SUBLANES
    for k in range(FFN_GROUPS):
        rows = slice(k * SUBLANES, (k + 1) * SUBLANES)
        s, j0 = k // groups_per_sublane, (k % groups_per_sublane) * SUBLANES
        y = x_ref[rows, :] + gather(j0 * SUBLANES + s, SUBLANES)
        o_ref[rows, :] = _rms(y, fw_ref[...]) if final_norm else y


def _ffn(x, nw, w_up, cw, cb, w_down, fw, final_norm):
    n_tok = x.shape[0]
    halo = T_FFN // SUBLANES
    last_halo = n_tok // SUBLANES - 1

    def resident(a):
        return pl.BlockSpec(a.shape, lambda i: (0,) * a.ndim, pipeline_mode=pl.Buffered(1))

    return pl.pallas_call(
        functools.partial(_ffn_kernel, final_norm=final_norm),
        grid=(n_tok // T_FFN,),
        in_specs=[
            pl.BlockSpec((T_FFN, D_MODEL), lambda i: (i, 0)),
            pl.BlockSpec((SUBLANES, D_MODEL), lambda i: (jnp.maximum(i * halo - 1, 0), 0)),
            pl.BlockSpec((SUBLANES, D_MODEL), lambda i: (jnp.minimum((i + 1) * halo, last_halo), 0)),
            resident(nw), resident(w_up), resident(cw), resident(cb), resident(w_down), resident(fw),
        ],
        out_specs=pl.BlockSpec((T_FFN, D_MODEL), lambda i: (i, 0)),
        out_shape=jax.ShapeDtypeStruct((n_tok, D_MODEL), F32),
        scratch_shapes=[pltpu.VMEM((T_FFN + 2 * SUBLANES, D_MODEL), BF16),
                        pltpu.VMEM((D_MODEL // LANES, SUBLANES * SLAB_PITCH, LANES), F32)],
        compiler_params=pltpu.CompilerParams(
            dimension_semantics=("arbitrary",), vmem_limit_bytes=VMEM_LIMIT),
        name="ffn",
    )(x, x, x, nw, w_up, cw, cb, w_down, fw)


def _layout_w_in(w_in_l):
    (q_a, zf_a, zb_a, i_a, og_a, q_b, k_b, v_b, og_b, lrf, lrb, x_c, y_c, g_a, g_b, g_c) = jnp.split(
        w_in_l.astype(BF16),
        [512, 1024, 1536, 2048, 2560, 2816, 3072, 3584, 4096, 4112, 4128, 4640, 5152, 6176, 7200], axis=1)
    pad = jnp.zeros((D_MODEL, LANES - 2 * GLA_RANK), BF16)
    q_b = q_b * GLA_Q_SCALE
    return jnp.concatenate([q_a, zf_a, zb_a, i_a, og_a, q_b, k_b, v_b, og_b, x_c, y_c, g_a, g_b, g_c,
                            lrf, lrb, pad], axis=1)


def _layout_gla_up(up_w_l):
    out = jnp.zeros((2, LANES, B_KEY), F32)
    out = out.at[0, 0:GLA_RANK].set(up_w_l[0])
    out = out.at[1, GLA_RANK:2 * GLA_RANK].set(up_w_l[1])
    return out.astype(BF16)


def _layout_rglru_gates(wa_l, wx_l):
    def tiles(w):
        w = w.reshape(2, 2, 4, C_BLOCK, C_BLOCK)
        eye = jnp.eye(4, dtype=w.dtype)
        t = jnp.einsum('dhbkj,bc->dhbkcj', w, eye)
        return t.reshape(2, 2, 4 * C_BLOCK, 4 * C_BLOCK)
    return jnp.stack([tiles(wa_l), tiles(wx_l)], axis=1).astype(BF16)


def kernel(x_prompt, x_sample, norm_mix_w, w_in, hgrn_lb_logits, hgrn_norm_w, gla_up_w, gla_up_b,
           gla_norm_w, c_conv_w, c_conv_b, rglru_wa, rglru_ba, rglru_wx, rglru_bx, rglru_lam,
           w_branch, w_out, norm_ffn_w, ffn_up, ffn_conv_w, ffn_conv_b, ffn_down, final_norm_w):
    lb_all = jnp.cumsum(jax.nn.softmax(hgrn_lb_logits.astype(F32), axis=0), axis=0)
    lb_all = lb_all - lb_all[0]
    fw = final_norm_w.reshape(1, D_MODEL)

    xs = [x_prompt.reshape(-1, D_MODEL), x_sample.reshape(-1, D_MODEL)]
    for l in range(DEPTH):
        w_in_l = _layout_w_in(w_in[l])
        up_w_l = _layout_gla_up(gla_up_w[l])
        gate_w_l = _layout_rglru_gates(rglru_wa[l], rglru_wx[l])
        gate_b_l = jnp.stack([rglru_ba[l], rglru_bx[l]], axis=1)
        w_branch_l, w_out_l = w_branch[l].astype(BF16), w_out[l].astype(BF16)
        ffn_up_l, ffn_down_l = ffn_up[l].astype(BF16), ffn_down[l].astype(BF16)
        for n, x in enumerate(xs):
            p = _proj(x, norm_mix_w[l].reshape(1, D_MODEL), lb_all[l], w_in_l)
            oaf, oab, obf, obb = _bidir_gla(p, up_w_l, gla_up_b[l].reshape(2, 1, B_KEY))
            hcf, hcb = _rglru(p, c_conv_w[l], c_conv_b[l].reshape(1, WIDTH), gate_w_l, gate_b_l,
                              rglru_lam[l])
            x = _merge(x, p, oaf, oab, obf, obb, hcf, hcb,
                       hgrn_norm_w[l].reshape(1, HEAD_V), gla_norm_w[l].reshape(1, HEAD_V),
                       w_branch_l, w_out_l)
            xs[n] = _ffn(x, norm_ffn_w[l].reshape(1, D_MODEL), ffn_up_l, ffn_conv_w[l],
                         ffn_conv_b[l].reshape(1, 2 * D_FF), ffn_down_l, fw,
                         final_norm=(l == DEPTH - 1))

    return (xs[0].reshape(x_prompt.shape), xs[1].reshape(x_sample.shape))
```

```python
import functools
import math

import jax
import jax.numpy as jnp
from jax import lax
from jax.experimental import pallas as pl
from jax.experimental.pallas import tpu as pltpu

F32 = jnp.float32
BF16 = jnp.bfloat16
P_DTYPE = BF16
O_DTYPE = BF16

D_MODEL = 1024
SEQ = 8192
DEPTH = 4
WIDTH = 512
HEAD_V = 128
A_KEY = 512
B_KEY = 256
GLA_RANK = 16
GLA_NORMALIZER = 16.0
GLA_PAIR = 2
GLA_Q_SCALE = 0.125
C_BLOCK = 64
C_CONV = 4
RG_C = 8.0
D_FF = 2816
FFN_CONV = 3
EPS = 1e-6

LANES = 128
SUBLANES = 8
HALO = 16
VMEM_LIMIT = 56 * 1024 * 1024

N_PROJ = 8320
COL_QA, COL_ZF, COL_ZB, COL_IA, COL_OGA = 0, 512, 1024, 1536, 2048
COL_QB, COL_KB, COL_VB, COL_OGB = 2560, 2816, 3072, 3584
COL_XC, COL_YC = 4096, 4608
COL_GA, COL_GB, COL_GC = 5120, 6144, 7168
COL_LR = 8192

SUB = 16
CHUNK = 64
N_SUB = CHUNK // SUB
PHASE_LAG = 4
T_MIX = 1024
MIX_UNROLL = 8
MIX_GROUPS = T_MIX // SUBLANES
MIX_PITCH = MIX_GROUPS + SUBLANES
T_PROJ = 512
N_PROJ_TILE = 2048
T_MERGE = 512
T_FFN = 512
FFN_UP_AHEAD = 11
FFN_GROUPS = T_FFN // SUBLANES
SLAB_PITCH = FFN_GROUPS + SUBLANES
F_TILE = 256


def _sigmoid(x):
    return 1.0 / (1.0 + jnp.exp(-x))


def _sigmoid_tanh(x):
    return 0.5 + 0.5 * jnp.tanh(0.5 * x)


def _gelu_tanh(x):
    return 0.5 * x * (1.0 + jnp.tanh(math.sqrt(2.0 / math.pi) * (x + 0.044715 * (x * x * x))))


def _rms(x, w):
    ms = jnp.mean(x * x, axis=-1, keepdims=True)
    return x * lax.rsqrt(ms + EPS) * w


def _proj_activation(col, u, lb_ref):
    if col == COL_QA:
        return u * _sigmoid_tanh(u) * (HEAD_V ** -0.5)
    if col in (COL_ZF, COL_ZB):
        lb = lb_ref[(col - COL_ZF) // WIDTH:(col - COL_ZF) // WIDTH + 1, :]
        return jnp.log(lb + (1.0 - lb) * _sigmoid(u))
    if col in (COL_OGA, COL_OGB):
        return u * _sigmoid_tanh(u)
    if col == COL_YC:
        return _gelu_tanh(u)
    if COL_GA <= col < COL_LR:
        return _sigmoid_tanh(u)
    return u


def _proj_kernel(x_ref, nw_ref, lb_ref, w_ref, o_ref):
    h = _rms(x_ref[...], nw_ref[...]).astype(BF16)
    for lo in range(0, N_PROJ, N_PROJ_TILE):
        hi = min(lo + N_PROJ_TILE, N_PROJ)
        u = jnp.dot(h, w_ref[:, lo:hi], preferred_element_type=F32)
        for col in range(lo, hi, WIDTH):
            end = min(col + WIDTH, hi)
            o_ref[:, col:end] = _proj_activation(col, u[:, col - lo:end - lo], lb_ref).astype(o_ref.dtype)


def _proj(x, nw, lb, w):
    n_tok = x.shape[0]
    return pl.pallas_call(
        _proj_kernel,
        grid=(n_tok // T_PROJ,),
        in_specs=[
            pl.BlockSpec((T_PROJ, D_MODEL), lambda i: (i, 0)),
            pl.BlockSpec((1, D_MODEL), lambda i: (0, 0), pipeline_mode=pl.Buffered(1)),
            pl.BlockSpec((2, WIDTH), lambda i: (0, 0), pipeline_mode=pl.Buffered(1)),
            pl.BlockSpec((D_MODEL, N_PROJ), lambda i: (0, 0), pipeline_mode=pl.Buffered(1)),
        ],
        out_specs=pl.BlockSpec((T_PROJ, N_PROJ), lambda i: (i, 0)),
        out_shape=jax.ShapeDtypeStruct((n_tok, N_PROJ), P_DTYPE),
        compiler_params=pltpu.CompilerParams(
            dimension_semantics=("arbitrary",), vmem_limit_bytes=VMEM_LIMIT),
        name="proj",
    )(x, nw, lb, w)


def _tri_consts(reverse):
    i = lax.broadcasted_iota(jnp.int32, (CHUNK, CHUNK), 0)
    j = lax.broadcasted_iota(jnp.int32, (CHUNK, CHUNK), 1)
    causal = (j >= i) if reverse else (j <= i)
    same_sub = (i // SUB) == (j // SUB)
    return jnp.where(causal & same_sub, 1.0, 0.0).astype(BF16), causal


def _chunk_cumsum(lg, cum_mat):
    lg_hi = lg.astype(BF16)
    lg_lo = (lg - lg_hi.astype(F32)).astype(BF16)
    return jnp.dot(cum_mat, jnp.concatenate([lg_hi, lg_lo], axis=1), preferred_element_type=F32)


def _chunk_scores(cum, q, k, v_list, st_list, head_masks, reverse):
    n_heads = len(v_list)
    b_loc = cum[:, :LANES] + cum[:, LANES:]
    order = range(N_SUB - 1, -1, -1) if reverse else range(N_SUB)
    offsets, total = [None] * N_SUB, jnp.zeros((1, LANES), F32)
    for s in order:
        offsets[s] = total
        edge = s * SUB if reverse else (s + 1) * SUB - 1
        total = total + b_loc[edge:edge + 1, :]
    r = jnp.concatenate([jnp.broadcast_to(o, (SUB, LANES)) for o in offsets], axis=0)
    b = b_loc + r
    q_loc = q * jnp.exp(b_loc)
    k_inv = k * jnp.exp(-b_loc)
    q_dec = (q_loc * jnp.exp(r)).astype(BF16)
    last = 0 if reverse else CHUNK - 1
    b_last = b[last:last + 1, :]
    k_tail = (k * jnp.exp(b_last - b)).astype(BF16)
    e_last = jnp.exp(b_last)

    q_loc_bf = q_loc.astype(BF16)
    k_groups = []
    q_groups = []
    for s in range(N_SUB):
        lo, hi = (s * SUB, CHUNK) if reverse else (0, (s + 1) * SUB)
        r_s = r[s * SUB:s * SUB + 1, :]
        kt = (k_inv[lo:hi] * jnp.exp(r_s - r[lo:hi])).astype(BF16)
        pieces = []
        if lo > 0:
            pieces.append(jnp.zeros((lo, LANES), BF16))
        pieces.append(kt)
        if hi < CHUNK:
            pieces.append(jnp.zeros((CHUNK - hi, LANES), BF16))
        k_groups.append(jnp.concatenate(pieces, axis=0) if len(pieces) > 1 else kt)
        qp = []
        if s > 0:
            qp.append(jnp.zeros((s * SUB, LANES), BF16))
        qp.append(q_loc_bf[s * SUB:(s + 1) * SUB])
        if s < N_SUB - 1:
            qp.append(jnp.zeros((CHUNK - (s + 1) * SUB, LANES), BF16))
        q_groups.append(jnp.concatenate(qp, axis=0))
    k_wide = jnp.concatenate(k_groups, axis=1)
    q_wide = jnp.concatenate(q_groups, axis=1)
    if head_masks[0] is None:
        q_stack = q_wide
    else:
        zero = jnp.zeros_like(q_wide)
        q_stack = jnp.concatenate([jnp.where(m[0], q_wide, zero) for m in head_masks], axis=0)
    att = lax.dot_general(q_stack, k_wide, (((1,), (1,)), ((), ())), preferred_element_type=F32)

    o_inter, upd = [], []
    for h in range(n_heads):
        qd = q_dec if head_masks[h] is None else jnp.where(head_masks[h][1], q_dec, jnp.zeros_like(q_dec))
        o_inter.append(lax.dot_general(qd, st_list[h].astype(BF16), (((1,), (1,)), ((), ())),
                                       preferred_element_type=F32))
        upd.append(lax.dot_general(v_list[h], k_tail, (((0,), (0,)), ((), ())),
                                   preferred_element_type=F32))
    return att, o_inter, upd, e_last


def _chunk_outputs(att, o_inter, upd, e_last, v_list, st_list, causal):
    outs, new_states = [], []
    for h in range(len(v_list)):
        att_h = jnp.where(causal, att[h * CHUNK:(h + 1) * CHUNK], 0.0).astype(BF16)
        outs.append(jnp.dot(att_h, v_list[h], preferred_element_type=F32) + o_inter[h])
        new_states.append(st_list[h] * e_last + upd[h])
    return outs, new_states


def _gla_kernel(aqf_ref, azf_ref, avf_ref, aqb_ref, azb_ref, avb_ref,
                bqf_ref, bkf_ref, bvf_ref, blrf_ref, bqb_ref, bkb_ref, bvb_ref, blrb_ref, upw_ref, upb_ref,
                oaf_ref, oab_ref, obf_ref, obb_ref, sta_ref, stb_ref, lg_ref):
    n_heads = WIDTH // HEAD_V

    @pl.when(pl.program_id(1) == 0)
    def _():
        sta_ref[...] = jnp.zeros_like(sta_ref)
        stb_ref[...] = jnp.zeros_like(stb_ref)

    for d, lr_ref in enumerate((blrf_ref, blrb_ref)):
        pre = jnp.dot(lr_ref[...].astype(BF16), upw_ref[d], preferred_element_type=F32) + upb_ref[d]
        log_sig = jnp.minimum(pre, 0.0) - jnp.log(1.0 + jnp.exp(-jnp.abs(pre)))
        lg_ref[d] = log_sig / GLA_NORMALIZER

    consts = [_tri_consts(False), _tri_consts(True)]
    width = LANES // GLA_PAIR
    lane_w = lax.broadcasted_iota(jnp.int32, (CHUNK, N_SUB * LANES), 1) % LANES
    lane = lax.broadcasted_iota(jnp.int32, (CHUNK, LANES), 1)
    pair_masks = [tuple((ln >= h * width) & (ln < (h + 1) * width) for ln in (lane_w, lane))
                  for h in range(GLA_PAIR)]
    n_chunks = T_MIX // CHUNK

    hgrn = [("a", d, g) for g in range(A_KEY // LANES) for d in range(2)]
    gla = [("b", d, g) for g in range(B_KEY // LANES) for d in range(2)]
    chains = []
    while hgrn or gla:
        chains += hgrn[:2] + gla[:1]
        hgrn, gla = hgrn[2:], gla[1:]

    assert len(chains) > PHASE_LAG
    sequence = [(u,) + chain for u in range(MIX_UNROLL) for chain in chains]

    def body(trip, carry):
        work = [dict() for _ in sequence]

        def phase1(w, u, mixer, d, g):
            reverse = d == 1
            c = trip * MIX_UNROLL + u
            cc = (n_chunks - 1 - c) if reverse else c
            w["rows"] = rows = pl.ds(pl.multiple_of(cc * CHUNK, CHUNK), CHUNK)
            cols = slice(g * LANES, (g + 1) * LANES)
            if mixer == "a":
                w["q"] = (aqb_ref if reverse else aqf_ref)[rows, cols].astype(F32)
                lg = (azb_ref if reverse else azf_ref)[rows, cols].astype(F32)
                w["k"] = 1.0 - jnp.exp(lg)
            else:
                w["q"] = (bqb_ref if reverse else bqf_ref)[rows, cols].astype(F32)
                w["k"] = (bkb_ref if reverse else bkf_ref)[rows, cols].astype(F32)
                lg = lg_ref[d, rows, cols]
            w["cum"] = _chunk_cumsum(lg, consts[d][0])

        def phase2(w, u, mixer, d, g):
            if mixer == "a":
                v_ref, st_ref, hs, masks = (avb_ref if d == 1 else avf_ref), sta_ref, [g], [None]
            else:
                v_ref, st_ref = (bvb_ref if d == 1 else bvf_ref), stb_ref
                hs, masks = [g * GLA_PAIR + h for h in range(GLA_PAIR)], pair_masks
            w["hs"] = hs
            w["v"] = [v_ref[w["rows"], h * HEAD_V:(h + 1) * HEAD_V].astype(BF16) for h in hs]
            w["st"] = [st_ref[d * n_heads + h] for h in hs]
            w["scores"] = _chunk_scores(w["cum"], w["q"], w["k"], w["v"], w["st"], masks, d == 1)

        def phase3(w, u, mixer, d, g):
            if mixer == "a":
                o_ref, st_ref = (oab_ref if d == 1 else oaf_ref), sta_ref
            else:
                o_ref, st_ref = (obb_ref if d == 1 else obf_ref), stb_ref
            outs, new_states = _chunk_outputs(*w["scores"], w["v"], w["st"], consts[d][1])
            for h, o, st in zip(w["hs"], outs, new_states):
                o_ref[w["rows"], h * HEAD_V:(h + 1) * HEAD_V] = o.astype(o_ref.dtype)
                st_ref[d * n_heads + h] = st

        phases = (phase1, phase2, phase3)
        for slot in range(len(sequence) + PHASE_LAG * (len(phases) - 1)):
            for p, phase in enumerate(phases):
                n = slot - p * PHASE_LAG
                if 0 <= n < len(sequence):
                    phase(work[n], *sequence[n])
        return carry

    lax.fori_loop(0, n_chunks // MIX_UNROLL, body, 0)


def _bidir_gla(p, upw, upb):
    n_tok = p.shape[0]
    n_seq = n_tok // SEQ
    nb = SEQ // T_MIX
    fwd = lambda s, i: s * nb + i
    bwd = lambda s, i: s * nb + (nb - 1 - i)

    def col(width, offset, blk):
        return pl.BlockSpec((T_MIX, width), lambda s, i: (blk(s, i), offset // width))

    def whole(a):
        return pl.BlockSpec(a.shape, lambda s, i: (0,) * a.ndim)

    in_specs = [col(WIDTH, COL_QA, fwd), col(WIDTH, COL_ZF, fwd), col(WIDTH, COL_IA, fwd),
                col(WIDTH, COL_QA, bwd), col(WIDTH, COL_ZB, bwd), col(WIDTH, COL_IA, bwd),
                col(B_KEY, COL_QB, fwd), col(B_KEY, COL_KB, fwd), col(WIDTH, COL_VB, fwd),
                col(LANES, COL_LR, fwd),
                col(B_KEY, COL_QB, bwd), col(B_KEY, COL_KB, bwd), col(WIDTH, COL_VB, bwd),
                col(LANES, COL_LR, bwd), whole(upw), whole(upb)]
    out_spec_f = pl.BlockSpec((T_MIX, WIDTH), lambda s, i: (fwd(s, i), 0))
    out_spec_b = pl.BlockSpec((T_MIX, WIDTH), lambda s, i: (bwd(s, i), 0))
    n_states = 2 * WIDTH // HEAD_V
    return pl.pallas_call(
        _gla_kernel,
        grid=(n_seq, nb),
        in_specs=in_specs,
        out_specs=[out_spec_f, out_spec_b, out_spec_f, out_spec_b],
        out_shape=[jax.ShapeDtypeStruct((n_tok, WIDTH), O_DTYPE)] * 4,
        scratch_shapes=[pltpu.VMEM((n_states, HEAD_V, LANES), F32), pltpu.VMEM((n_states, HEAD_V, LANES), F32),
                        pltpu.VMEM((2, T_MIX, B_KEY), F32)],
        compiler_params=pltpu.CompilerParams(
            dimension_semantics=("arbitrary", "arbitrary"), vmem_limit_bytes=VMEM_LIMIT),
        name="mix_gla",
    )(*([p] * 14), upw, upb)


def _rglru_kernel(xf_ref, xfp_ref, xfn_ref, xb_ref, xbp_ref, xbn_ref, cw_ref, cb_ref, gw_ref, gb_ref,
                  lam_ref, of_ref, ob_ref, slab_ref, a_ref, u_ref, carry_ref):
    i = pl.program_id(1)
    nb = pl.num_programs(1)

    @pl.when(i == 0)
    def _():
        carry_ref[...] = jnp.zeros_like(carry_ref)

    n_slabs = WIDTH // LANES
    top = T_MIX - SUBLANES

    def gather(start, stride):
        return jnp.concatenate([slab_ref[c, pl.ds(start, SUBLANES, stride=stride), :]
                                for c in range(n_slabs)], axis=1)

    for d in range(2):
        reverse = d == 1
        x_ref, xp_ref, xn_ref, o_ref = (xb_ref, xbp_ref, xbn_ref, ob_ref) if reverse else (
            xf_ref, xfp_ref, xfn_ref, of_ref)
        blk = (nb - 1 - i) if reverse else i
        xf = x_ref[...].astype(F32)
        for c in range(n_slabs):
            for s in range(SUBLANES):
                slab_ref[c, s * MIX_PITCH:s * MIX_PITCH + MIX_GROUPS, :] = xf[
                    s * MIX_GROUPS:(s + 1) * MIX_GROUPS, c * LANES:(c + 1) * LANES]
        groups = [gather(j, MIX_PITCH) for j in range(MIX_GROUPS)]
        x = jnp.concatenate(groups, axis=0)
        before = jnp.where(blk == 0, 0.0, xp_ref[...].astype(F32))
        after = jnp.where(blk == nb - 1, 0.0, xn_ref[...].astype(F32))
        down1 = jnp.concatenate([before[HALO - 1:HALO], groups[-1][:SUBLANES - 1]], axis=0)
        down2 = jnp.concatenate([before[HALO - 2:HALO - 1], groups[-2][:SUBLANES - 1]], axis=0)
        up1 = jnp.concatenate([groups[0][1:], after[0:1]], axis=0)
        x_m1 = jnp.concatenate([down1, x[:top]], axis=0)
        x_m2 = jnp.concatenate([down2, down1, x[:top - SUBLANES]], axis=0)
        x_p1 = jnp.concatenate([x[SUBLANES:], up1], axis=0)
        xc = (cb_ref[...] + x_m2 * cw_ref[0:1, :] + x_m1 * cw_ref[1:2, :] + x * cw_ref[2:3, :]
              + x_p1 * cw_ref[3:4, :])
        xc_bf = xc.astype(BF16)
        gates = []
        for g in range(2):
            halves = [jnp.dot(xc_bf[:, hh * 256:(hh + 1) * 256], gw_ref[d, g, hh],
                              preferred_element_type=F32) for hh in range(2)]
            gates.append(_sigmoid_tanh(jnp.concatenate(halves, axis=1) + gb_ref[d, g:g + 1, :]))
        r_gate, i_gate = gates
        lam = lam_ref[d:d + 1, :]
        softplus_neg_lam = jnp.maximum(-lam, 0.0) + jnp.log(1.0 + jnp.exp(-jnp.abs(lam)))
        log_a = -RG_C * r_gate * softplus_neg_lam
        a = jnp.exp(log_a)
        one_minus_a2 = -jnp.tanh(log_a) * (1.0 + a * a)
        root = jnp.where(one_minus_a2 > 0.0, one_minus_a2 * lax.rsqrt(one_minus_a2), 0.0)
        u = root * (i_gate * xc)
        a_ref[...] = a
        u_ref[...] = u

        def body(n, state):
            h_prev, p_prev = state
            j = (MIX_GROUPS - 1 - n) if reverse else n
            rows = pl.ds(pl.multiple_of(j * SUBLANES, SUBLANES), SUBLANES)
            a_j = a_ref[rows, :]
            h = a_j * h_prev + u_ref[rows, :]
            p = a_j * p_prev
            u_ref[rows, :] = h
            a_ref[rows, :] = p
            return h, p

        init = (jnp.zeros((SUBLANES, WIDTH), F32), jnp.ones((SUBLANES, WIDTH), F32))
        h_end, p_end = lax.fori_loop(0, MIX_GROUPS, body, init, unroll=8)
        carry = carry_ref[d][0:1, :]
        carries = [None] * SUBLANES
        for s in (range(SUBLANES - 1, -1, -1) if reverse else range(SUBLANES)):
            carries[s] = carry
            carry = h_end[s:s + 1, :] + p_end[s:s + 1, :] * carry
        carry_ref[d] = jnp.broadcast_to(carry, (SUBLANES, WIDTH))
        carry_in = jnp.concatenate(carries, axis=0)
        h_all = u_ref[...] + a_ref[...] * jnp.concatenate([carry_in] * MIX_GROUPS, axis=0)
        for c in range(n_slabs):
            slab_ref[c, 0:T_MIX, :] = h_all[:, c * LANES:(c + 1) * LANES]
        groups_per_sublane = MIX_GROUPS // SUBLANES
        for k2 in range(MIX_GROUPS // 2):
            pair = []
            for k in (2 * k2, 2 * k2 + 1):
                s, j0 = k // groups_per_sublane, (k % groups_per_sublane) * SUBLANES
                pair.append(gather(j0 * SUBLANES + s, SUBLANES))
            o_ref[2 * SUBLANES * k2:2 * SUBLANES * (k2 + 1), :] = jnp.concatenate(pair, axis=0).astype(
                o_ref.dtype)


def _rglru(p, cw, cb, gw, gb, lam):
    n_tok = p.shape[0]
    n_seq = n_tok // SEQ
    nb = SEQ // T_MIX
    halo = T_MIX // HALO
    last_halo = n_tok // HALO - 1
    xcol = COL_XC // WIDTH
    fwd = lambda s, i: s * nb + i
    bwd = lambda s, i: s * nb + (nb - 1 - i)

    def specs(blk):
        return [
            pl.BlockSpec((T_MIX, WIDTH), lambda s, i: (blk(s, i), xcol)),
            pl.BlockSpec((HALO, WIDTH), lambda s, i: (jnp.maximum(blk(s, i) * halo - 1, 0), xcol)),
            pl.BlockSpec((HALO, WIDTH),
                         lambda s, i: (jnp.minimum((blk(s, i) + 1) * halo, last_halo), xcol)),
        ]

    def whole(a):
        return pl.BlockSpec(a.shape, lambda s, i: (0,) * a.ndim)

    return pl.pallas_call(
        _rglru_kernel,
        grid=(n_seq, nb),
        in_specs=specs(fwd) + specs(bwd) + [whole(cw), whole(cb), whole(gw), whole(gb), whole(lam)],
        out_specs=[pl.BlockSpec((T_MIX, WIDTH), lambda s, i: (fwd(s, i), 0)),
                   pl.BlockSpec((T_MIX, WIDTH), lambda s, i: (bwd(s, i), 0))],
        out_shape=[jax.ShapeDtypeStruct((n_tok, WIDTH), O_DTYPE)] * 2,
        scratch_shapes=[pltpu.VMEM((WIDTH // LANES, SUBLANES * MIX_PITCH, LANES), F32),
                        pltpu.VMEM((T_MIX, WIDTH), F32),
                        pltpu.VMEM((T_MIX, WIDTH), F32),
                        pltpu.VMEM((2, SUBLANES, WIDTH), F32)],
        compiler_params=pltpu.CompilerParams(
            dimension_semantics=("arbitrary", "arbitrary"), vmem_limit_bytes=VMEM_LIMIT),
        name="mix_rglru",
    )(p, p, p, p, p, p, cw, cb, gw, gb, lam)


def _head_norm(o, w):
    parts = [_rms(o[:, h * HEAD_V:(h + 1) * HEAD_V], w) for h in range(WIDTH // HEAD_V)]
    return jnp.concatenate(parts, axis=1)


def _merge_kernel(x_ref, oaf_ref, oab_ref, oga_ref, obf_ref, obb_ref, ogb_ref, hcf_ref, hcb_ref, yc_ref,
                  ga_ref, gb_ref, gc_ref, nwa_ref, nwb_ref, wbr_ref, wout_ref, o_ref):
    ld = lambda ref: ref[...].astype(F32)
    o_a = _head_norm(ld(oaf_ref) + ld(oab_ref), nwa_ref[...]) * ld(oga_ref)
    o_b = _head_norm(ld(obf_ref) + ld(obb_ref), nwb_ref[...]) * ld(ogb_ref)
    o_c = (ld(hcf_ref) + ld(hcb_ref)) * ld(yc_ref)
    m = ld(ga_ref) * jnp.dot(o_a.astype(BF16), wbr_ref[0], preferred_element_type=F32)
    m = m + ld(gb_ref) * jnp.dot(o_b.astype(BF16), wbr_ref[1], preferred_element_type=F32)
    m = m + ld(gc_ref) * jnp.dot(o_c.astype(BF16), wbr_ref[2], preferred_element_type=F32)
    o_ref[...] = x_ref[...] + jnp.dot(m.astype(BF16), wout_ref[...], preferred_element_type=F32)


def _merge(x, p, oaf, oab, obf, obb, hcf, hcb, nwa, nwb, wbr, wout):
    n_tok = x.shape[0]
    tok = lambda width, offset: pl.BlockSpec((T_MERGE, width), lambda i: (i, offset // width))

    def whole(a):
        return pl.BlockSpec(a.shape, lambda i: (0,) * a.ndim, pipeline_mode=pl.Buffered(1))

    return pl.pallas_call(
        _merge_kernel,
        grid=(n_tok // T_MERGE,),
        in_specs=[tok(D_MODEL, 0), tok(WIDTH, 0), tok(WIDTH, 0), tok(WIDTH, COL_OGA),
                  tok(WIDTH, 0), tok(WIDTH, 0), tok(WIDTH, COL_OGB),
                  tok(WIDTH, 0), tok(WIDTH, 0), tok(WIDTH, COL_YC),
                  tok(D_MODEL, COL_GA), tok(D_MODEL, COL_GB), tok(D_MODEL, COL_GC),
                  whole(nwa), whole(nwb), whole(wbr), whole(wout)],
        out_specs=pl.BlockSpec((T_MERGE, D_MODEL), lambda i: (i, 0)),
        out_shape=jax.ShapeDtypeStruct((n_tok, D_MODEL), F32),
        compiler_params=pltpu.CompilerParams(
            dimension_semantics=("arbitrary",), vmem_limit_bytes=VMEM_LIMIT),
        name="merge",
    )(x, oaf, oab, p, obf, obb, p, hcf, hcb, p, p, p, p, nwa, nwb, wbr, wout)


def _ffn_kernel(x_ref, xp_ref, xn_ref, nw_ref, wup_ref, cw_ref, cb_ref, wd_ref, fw_ref, o_ref, h_ref,
                slab_ref, *, final_norm):
    i = pl.program_id(0)
    blocks_per_seq = SEQ // T_FFN
    first = (i % blocks_per_seq) == 0
    last = (i % blocks_per_seq) == blocks_per_seq - 1
    nw = nw_ref[...]

    n_slabs = D_MODEL // LANES

    for c in range(n_slabs):
        for s in range(SUBLANES):
            slab_ref[c, s * SLAB_PITCH:s * SLAB_PITCH + FFN_GROUPS, :] = x_ref[
                s * FFN_GROUPS:(s + 1) * FFN_GROUPS, c * LANES:(c + 1) * LANES]

    def gather(start, stride):
        return jnp.concatenate([slab_ref[c, pl.ds(start, SUBLANES, stride=stride), :]
                                for c in range(n_slabs)], axis=1)

    for jj in range(FFN_GROUPS // 2):
        xb = jnp.concatenate([gather(2 * jj, SLAB_PITCH), gather(2 * jj + 1, SLAB_PITCH)], axis=0)
        h_ref[2 * SUBLANES * jj:2 * SUBLANES * (jj + 1), :] = _rms(xb, nw).astype(BF16)
    halo = jnp.concatenate([jnp.where(first, 0.0, _rms(xp_ref[...], nw)),
                            jnp.where(last, 0.0, _rms(xn_ref[...], nw))], axis=0)
    h_ref[T_FFN:, :] = halo.astype(BF16)

    def up(f):
        h = h_ref[...]
        return tuple(jnp.dot(h, wup_ref[:, c * D_FF + f * F_TILE:c * D_FF + (f + 1) * F_TILE],
                             preferred_element_type=F32) for c in range(2))

    def conv(u, c, f):
        cols = slice(c * D_FF + f * F_TILE, c * D_FF + (f + 1) * F_TILE)
        body = u[:T_FFN]
        before = u[T_FFN + SUBLANES - 1:T_FFN + SUBLANES]
        after = u[T_FFN + SUBLANES:T_FFN + SUBLANES + 1]
        prev0 = jnp.concatenate([before, body[T_FFN - SUBLANES:T_FFN - 1]], axis=0)
        next_last = jnp.concatenate([body[1:SUBLANES], after], axis=0)
        u_prev = jnp.concatenate([prev0, body[:T_FFN - SUBLANES]], axis=0)
        u_next = jnp.concatenate([body[SUBLANES:], next_last], axis=0)
        return (cb_ref[:, cols] + u_prev * cw_ref[0:1, cols] + body * cw_ref[1:2, cols]
                + u_next * cw_ref[2:3, cols])

    n_f = D_FF // F_TILE
    acc = None
    pending = [up(f) for f in range(FFN_UP_AHEAD)]
    for f in range(n_f):
        ug, uv = pending.pop(0)
        if f + FFN_UP_AHEAD < n_f:
            pending.append(up(f + FFN_UP_AHEAD))
        act = _gelu_tanh(conv(ug, 0, f)) * conv(uv, 1, f)
        down = jnp.dot(act.astype(BF16), wd_ref[f * F_TILE:(f + 1) * F_TILE, :],
                       preferred_element_type=F32)
        acc = down if acc is None else acc + down

    for c in range(n_slabs):
        slab_ref[c, 0:T_FFN, :] = acc[:, c * LANES:(c + 1) * LANES]
    groups_per_sublane = FFN_GROUPS // SUBLANES
    for k in range(FFN_GROUPS):
        rows = slice(k * SUBLANES, (k + 1) * SUBLANES)
        s, j0 = k // groups_per_sublane, (k % groups_per_sublane) * SUBLANES
        y = x_ref[rows, :] + gather(j0 * SUBLANES + s, SUBLANES)
        o_ref[rows, :] = _rms(y, fw_ref[...]) if final_norm else y


def _ffn(x, nw, w_up, cw, cb, w_down, fw, final_norm):
    n_tok = x.shape[0]
    halo = T_FFN // SUBLANES
    last_halo = n_tok // SUBLANES - 1

    def resident(a):
        return pl.BlockSpec(a.shape, lambda i: (0,) * a.ndim, pipeline_mode=pl.Buffered(1))

    return pl.pallas_call(
        functools.partial(_ffn_kernel, final_norm=final_norm),
        grid=(n_tok // T_FFN,),
        in_specs=[
            pl.BlockSpec((T_FFN, D_MODEL), lambda i: (i, 0)),
            pl.BlockSpec((SUBLANES, D_MODEL), lambda i: (jnp.maximum(i * halo - 1, 0), 0)),
            pl.BlockSpec((SUBLANES, D_MODEL), lambda i: (jnp.minimum((i + 1) * halo, last_halo), 0)),
            resident(nw), resident(w_up), resident(cw), resident(cb), resident(w_down), resident(fw),
        ],
        out_specs=pl.BlockSpec((T_FFN, D_MODEL), lambda i: (i, 0)),
        out_shape=jax.ShapeDtypeStruct((n_tok, D_MODEL), F32),
        scratch_shapes=[pltpu.VMEM((T_FFN + 2 * SUBLANES, D_MODEL), BF16),
                        pltpu.VMEM((D_MODEL // LANES, SUBLANES * SLAB_PITCH, LANES), F32)],
        compiler_params=pltpu.CompilerParams(
            dimension_semantics=("arbitrary",), vmem_limit_bytes=VMEM_LIMIT),
        name="ffn",
    )(x, x, x, nw, w_up, cw, cb, w_down, fw)


def _layout_w_in(w_in_l):
    (q_a, zf_a, zb_a, i_a, og_a, q_b, k_b, v_b, og_b, lrf, lrb, x_c, y_c, g_a, g_b, g_c) = jnp.split(
        w_in_l.astype(BF16),
        [512, 1024, 1536, 2048, 2560, 2816, 3072, 3584, 4096, 4112, 4128, 4640, 5152, 6176, 7200], axis=1)
    pad = jnp.zeros((D_MODEL, LANES - 2 * GLA_RANK), BF16)
    q_b = q_b * GLA_Q_SCALE
    return jnp.concatenate([q_a, zf_a, zb_a, i_a, og_a, q_b, k_b, v_b, og_b, x_c, y_c, g_a, g_b, g_c,
                            lrf, lrb, pad], axis=1)


def _layout_gla_up(up_w_l):
    out = jnp.zeros((2, LANES, B_KEY), F32)
    out = out.at[0, 0:GLA_RANK].set(up_w_l[0])
    out = out.at[1, GLA_RANK:2 * GLA_RANK].set(up_w_l[1])
    return out.astype(BF16)


def _layout_rglru_gates(wa_l, wx_l):
    def tiles(w):
        w = w.reshape(2, 2, 4, C_BLOCK, C_BLOCK)
        eye = jnp.eye(4, dtype=w.dtype)
        t = jnp.einsum('dhbkj,bc->dhbkcj', w, eye)
        return t.reshape(2, 2, 4 * C_BLOCK, 4 * C_BLOCK)
    return jnp.stack([tiles(wa_l), tiles(wx_l)], axis=1).astype(BF16)


def kernel(x_prompt, x_sample, norm_mix_w, w_in, hgrn_lb_logits, hgrn_norm_w, gla_up_w, gla_up_b,
           gla_norm_w, c_conv_w, c_conv_b, rglru_wa, rglru_ba, rglru_wx, rglru_bx, rglru_lam,
           w_branch, w_out, norm_ffn_w, ffn_up, ffn_conv_w, ffn_conv_b, ffn_down, final_norm_w):
    lb_all = jnp.cumsum(jax.nn.softmax(hgrn_lb_logits.astype(F32), axis=0), axis=0)
    lb_all = lb_all - lb_all[0]
    fw = final_norm_w.reshape(1, D_MODEL)

    xs = [x_prompt.reshape(-1, D_MODEL), x_sample.reshape(-1, D_MODEL)]
    for l in range(DEPTH):
        w_in_l = _layout_w_in(w_in[l])
        up_w_l = _layout_gla_up(gla_up_w[l])
        gate_w_l = _layout_rglru_gates(rglru_wa[l], rglru_wx[l])
        gate_b_l = jnp.stack([rglru_ba[l], rglru_bx[l]], axis=1)
        w_branch_l, w_out_l = w_branch[l].astype(BF16), w_out[l].astype(BF16)
        ffn_up_l, ffn_down_l = ffn_up[l].astype(BF16), ffn_down[l].astype(BF16)
        for n, x in enumerate(xs):
            p = _proj(x, norm_mix_w[l].reshape(1, D_MODEL), lb_all[l], w_in_l)
            oaf, oab, obf, obb = _bidir_gla(p, up_w_l, gla_up_b[l].reshape(2, 1, B_KEY))
            hcf, hcb = _rglru(p, c_conv_w[l], c_conv_b[l].reshape(1, WIDTH), gate_w_l, gate_b_l,
                              rglru_lam[l])
            x = _merge(x, p, oaf, oab, obf, obb, hcf, hcb,
                       hgrn_norm_w[l].reshape(1, HEAD_V), gla_norm_w[l].reshape(1, HEAD_V),
                       w_branch_l, w_out_l)
            xs[n] = _ffn(x, norm_ffn_w[l].reshape(1, D_MODEL), ffn_up_l, ffn_conv_w[l],
                         ffn_conv_b[l].reshape(1, 2 * D_FF), ffn_down_l, fw,
                         final_norm=(l == DEPTH - 1))

    return (xs[0].reshape(x_prompt.shape), xs[1].reshape(x_sample.shape))
```

```python
import functools
import math

import jax
import jax.numpy as jnp
from jax import lax
from jax.experimental import pallas as pl
from jax.experimental.pallas import tpu as pltpu

F32 = jnp.float32
BF16 = jnp.bfloat16
P_DTYPE = BF16
O_DTYPE = BF16

D_MODEL = 1024
SEQ = 8192
DEPTH = 4
WIDTH = 512
HEAD_V = 128
A_KEY = 512
B_KEY = 256
GLA_RANK = 16
GLA_NORMALIZER = 16.0
GLA_PAIR = 2
GLA_Q_SCALE = 0.125
C_BLOCK = 64
C_CONV = 4
RG_C = 8.0
D_FF = 2816
FFN_CONV = 3
EPS = 1e-6

LANES = 128
SUBLANES = 8
HALO = 16
VMEM_LIMIT = 56 * 1024 * 1024

N_PROJ = 8320
COL_QA, COL_ZF, COL_ZB, COL_IA, COL_OGA = 0, 512, 1024, 1536, 2048
COL_QB, COL_KB, COL_VB, COL_OGB = 2560, 2816, 3072, 3584
COL_XC, COL_YC = 4096, 4608
COL_GA, COL_GB, COL_GC = 5120, 6144, 7168
COL_LR = 8192

SUB = 16
CHUNK = 64
N_SUB = CHUNK // SUB
PHASE_LAG = 4
T_MIX = 1024
MIX_UNROLL = 8
MIX_GROUPS = T_MIX // SUBLANES
MIX_PITCH = MIX_GROUPS + SUBLANES
T_PROJ = 512
N_PROJ_TILE = 2048
T_MERGE = 512
T_FFN = 512
FFN_UP_AHEAD = 11
FFN_GROUPS = T_FFN // SUBLANES
SLAB_PITCH = FFN_GROUPS + SUBLANES
F_TILE = 256
FFN_DOWN_SPLITS = (0, 6, 11)


def _sigmoid(x):
    return 1.0 / (1.0 + jnp.exp(-x))


def _sigmoid_tanh(x):
    return 0.5 + 0.5 * jnp.tanh(0.5 * x)


def _gelu_tanh(x):
    return 0.5 * x * (1.0 + jnp.tanh(math.sqrt(2.0 / math.pi) * (x + 0.044715 * (x * x * x))))


def _rms(x, w):
    ms = jnp.mean(x * x, axis=-1, keepdims=True)
    return x * lax.rsqrt(ms + EPS) * w


def _proj_activation(col, u, lb_ref):
    if col == COL_QA:
        return u * _sigmoid_tanh(u) * (HEAD_V ** -0.5)
    if col in (COL_ZF, COL_ZB):
        lb = lb_ref[(col - COL_ZF) // WIDTH:(col - COL_ZF) // WIDTH + 1, :]
        return jnp.log(lb + (1.0 - lb) * _sigmoid(u))
    if col in (COL_OGA, COL_OGB):
        return u * _sigmoid_tanh(u)
    if col == COL_YC:
        return _gelu_tanh(u)
    if COL_GA <= col < COL_LR:
        return _sigmoid_tanh(u)
    return u


def _proj_kernel(x_ref, nw_ref, lb_ref, w_ref, o_ref):
    h = _rms(x_ref[...], nw_ref[...]).astype(BF16)
    for lo in range(0, N_PROJ, N_PROJ_TILE):
        hi = min(lo + N_PROJ_TILE, N_PROJ)
        u = jnp.dot(h, w_ref[:, lo:hi], preferred_element_type=F32)
        for col in range(lo, hi, WIDTH):
            end = min(col + WIDTH, hi)
            o_ref[:, col:end] = _proj_activation(col, u[:, col - lo:end - lo], lb_ref).astype(o_ref.dtype)


def _proj(x, nw, lb, w):
    n_tok = x.shape[0]
    return pl.pallas_call(
        _proj_kernel,
        grid=(n_tok // T_PROJ,),
        in_specs=[
            pl.BlockSpec((T_PROJ, D_MODEL), lambda i: (i, 0)),
            pl.BlockSpec((1, D_MODEL), lambda i: (0, 0), pipeline_mode=pl.Buffered(1)),
            pl.BlockSpec((2, WIDTH), lambda i: (0, 0), pipeline_mode=pl.Buffered(1)),
            pl.BlockSpec((D_MODEL, N_PROJ), lambda i: (0, 0), pipeline_mode=pl.Buffered(1)),
        ],
        out_specs=pl.BlockSpec((T_PROJ, N_PROJ), lambda i: (i, 0)),
        out_shape=jax.ShapeDtypeStruct((n_tok, N_PROJ), P_DTYPE),
        compiler_params=pltpu.CompilerParams(
            dimension_semantics=("arbitrary",), vmem_limit_bytes=VMEM_LIMIT),
        name="proj",
    )(x, nw, lb, w)


def _tri_consts(reverse):
    i = lax.broadcasted_iota(jnp.int32, (CHUNK, CHUNK), 0)
    j = lax.broadcasted_iota(jnp.int32, (CHUNK, CHUNK), 1)
    causal = (j >= i) if reverse else (j <= i)
    same_sub = (i // SUB) == (j // SUB)
    return jnp.where(causal & same_sub, 1.0, 0.0).astype(BF16), causal


def _chunk_cumsum(lg, cum_mat):
    lg_hi = lg.astype(BF16)
    lg_lo = (lg - lg_hi.astype(F32)).astype(BF16)
    return jnp.dot(cum_mat, jnp.concatenate([lg_hi, lg_lo], axis=1), preferred_element_type=F32)


def _chunk_scores(cum, q, k, v_list, st_list, head_masks, reverse):
    n_heads = len(v_list)
    b_loc = cum[:, :LANES] + cum[:, LANES:]
    order = range(N_SUB - 1, -1, -1) if reverse else range(N_SUB)
    offsets, total = [None] * N_SUB, jnp.zeros((1, LANES), F32)
    for s in order:
        offsets[s] = total
        edge = s * SUB if reverse else (s + 1) * SUB - 1
        total = total + b_loc[edge:edge + 1, :]
    r = jnp.concatenate([jnp.broadcast_to(o, (SUB, LANES)) for o in offsets], axis=0)
    b = b_loc + r
    q_loc = q * jnp.exp(b_loc)
    k_inv = k * jnp.exp(-b_loc)
    q_dec = (q_loc * jnp.exp(r)).astype(BF16)
    last = 0 if reverse else CHUNK - 1
    b_last = b[last:last + 1, :]
    k_tail = (k * jnp.exp(b_last - b)).astype(BF16)
    e_last = jnp.exp(b_last)

    q_loc_bf = q_loc.astype(BF16)
    k_groups = []
    q_groups = []
    for s in range(N_SUB):
        lo, hi = (s * SUB, CHUNK) if reverse else (0, (s + 1) * SUB)
        r_s = r[s * SUB:s * SUB + 1, :]
        kt = (k_inv[lo:hi] * jnp.exp(r_s - r[lo:hi])).astype(BF16)
        pieces = []
        if lo > 0:
            pieces.append(jnp.zeros((lo, LANES), BF16))
        pieces.append(kt)
        if hi < CHUNK:
            pieces.append(jnp.zeros((CHUNK - hi, LANES), BF16))
        k_groups.append(jnp.concatenate(pieces, axis=0) if len(pieces) > 1 else kt)
        qp = []
        if s > 0:
            qp.append(jnp.zeros((s * SUB, LANES), BF16))
        qp.append(q_loc_bf[s * SUB:(s + 1) * SUB])
        if s < N_SUB - 1:
            qp.append(jnp.zeros((CHUNK - (s + 1) * SUB, LANES), BF16))
        q_groups.append(jnp.concatenate(qp, axis=0))
    k_wide = jnp.concatenate(k_groups, axis=1)
    q_wide = jnp.concatenate(q_groups, axis=1)
    if head_masks[0] is None:
        q_stack = q_wide
    else:
        zero = jnp.zeros_like(q_wide)
        q_stack = jnp.concatenate([jnp.where(m[0], q_wide, zero) for m in head_masks], axis=0)
    att = lax.dot_general(q_stack, k_wide, (((1,), (1,)), ((), ())), preferred_element_type=F32)

    o_inter, upd = [], []
    for h in range(n_heads):
        qd = q_dec if head_masks[h] is None else jnp.where(head_masks[h][1], q_dec, jnp.zeros_like(q_dec))
        o_inter.append(lax.dot_general(qd, st_list[h].astype(BF16), (((1,), (1,)), ((), ())),
                                       preferred_element_type=F32))
        upd.append(lax.dot_general(v_list[h], k_tail, (((0,), (0,)), ((), ())),
                                   preferred_element_type=F32))
    return att, o_inter, upd, e_last


def _chunk_outputs(att, o_inter, upd, e_last, v_list, st_list, causal):
    outs, new_states = [], []
    for h in range(len(v_list)):
        att_h = jnp.where(causal, att[h * CHUNK:(h + 1) * CHUNK], 0.0).astype(BF16)
        outs.append(jnp.dot(att_h, v_list[h], preferred_element_type=F32) + o_inter[h])
        new_states.append(st_list[h] * e_last + upd[h])
    return outs, new_states


def _gla_kernel(aqf_ref, azf_ref, avf_ref, aqb_ref, azb_ref, avb_ref,
                bqf_ref, bkf_ref, bvf_ref, blrf_ref, bqb_ref, bkb_ref, bvb_ref, blrb_ref, upw_ref, upb_ref,
                oaf_ref, oab_ref, obf_ref, obb_ref, sta_ref, stb_ref, lg_ref):
    n_heads = WIDTH // HEAD_V

    @pl.when(pl.program_id(1) == 0)
    def _():
        sta_ref[...] = jnp.zeros_like(sta_ref)
        stb_ref[...] = jnp.zeros_like(stb_ref)

    for d, lr_ref in enumerate((blrf_ref, blrb_ref)):
        pre = jnp.dot(lr_ref[...].astype(BF16), upw_ref[d], preferred_element_type=F32) + upb_ref[d]
        log_sig = jnp.minimum(pre, 0.0) - jnp.log(1.0 + jnp.exp(-jnp.abs(pre)))
        lg_ref[d] = log_sig / GLA_NORMALIZER

    consts = [_tri_consts(False), _tri_consts(True)]
    width = LANES // GLA_PAIR
    lane_w = lax.broadcasted_iota(jnp.int32, (CHUNK, N_SUB * LANES), 1) % LANES
    lane = lax.broadcasted_iota(jnp.int32, (CHUNK, LANES), 1)
    pair_masks = [tuple((ln >= h * width) & (ln < (h + 1) * width) for ln in (lane_w, lane))
                  for h in range(GLA_PAIR)]
    n_chunks = T_MIX // CHUNK

    hgrn = [("a", d, g) for g in range(A_KEY // LANES) for d in range(2)]
    gla = [("b", d, g) for g in range(B_KEY // LANES) for d in range(2)]
    chains = []
    while hgrn or gla:
        chains += hgrn[:2] + gla[:1]
        hgrn, gla = hgrn[2:], gla[1:]

    assert len(chains) > PHASE_LAG
    sequence = [(u,) + chain for u in range(MIX_UNROLL) for chain in chains]

    def body(trip, carry):
        work = [dict() for _ in sequence]

        def phase1(w, u, mixer, d, g):
            reverse = d == 1
            c = trip * MIX_UNROLL + u
            cc = (n_chunks - 1 - c) if reverse else c
            w["rows"] = rows = pl.ds(pl.multiple_of(cc * CHUNK, CHUNK), CHUNK)
            cols = slice(g * LANES, (g + 1) * LANES)
            if mixer == "a":
                w["q"] = (aqb_ref if reverse else aqf_ref)[rows, cols].astype(F32)
                lg = (azb_ref if reverse else azf_ref)[rows, cols].astype(F32)
                w["k"] = 1.0 - jnp.exp(lg)
            else:
                w["q"] = (bqb_ref if reverse else bqf_ref)[rows, cols].astype(F32)
                w["k"] = (bkb_ref if reverse else bkf_ref)[rows, cols].astype(F32)
                lg = lg_ref[d, rows, cols]
            w["cum"] = _chunk_cumsum(lg, consts[d][0])

        def phase2(w, u, mixer, d, g):
            if mixer == "a":
                v_ref, st_ref, hs, masks = (avb_ref if d == 1 else avf_ref), sta_ref, [g], [None]
            else:
                v_ref, st_ref = (bvb_ref if d == 1 else bvf_ref), stb_ref
                hs, masks = [g * GLA_PAIR + h for h in range(GLA_PAIR)], pair_masks
            w["hs"] = hs
            w["v"] = [v_ref[w["rows"], h * HEAD_V:(h + 1) * HEAD_V].astype(BF16) for h in hs]
            w["st"] = [st_ref[d * n_heads + h] for h in hs]
            w["scores"] = _chunk_scores(w["cum"], w["q"], w["k"], w["v"], w["st"], masks, d == 1)

        def phase3(w, u, mixer, d, g):
            if mixer == "a":
                o_ref, st_ref = (oab_ref if d == 1 else oaf_ref), sta_ref
            else:
                o_ref, st_ref = (obb_ref if d == 1 else obf_ref), stb_ref
            outs, new_states = _chunk_outputs(*w["scores"], w["v"], w["st"], consts[d][1])
            for h, o, st in zip(w["hs"], outs, new_states):
                o_ref[w["rows"], h * HEAD_V:(h + 1) * HEAD_V] = o.astype(o_ref.dtype)
                st_ref[d * n_heads + h] = st

        phases = (phase1, phase2, phase3)
        for slot in range(len(sequence) + PHASE_LAG * (len(phases) - 1)):
            for p, phase in enumerate(phases):
                n = slot - p * PHASE_LAG
                if 0 <= n < len(sequence):
                    phase(work[n], *sequence[n])
        return carry

    lax.fori_loop(0, n_chunks // MIX_UNROLL, body, 0)


def _bidir_gla(p, upw, upb):
    n_tok = p.shape[0]
    n_seq = n_tok // SEQ
    nb = SEQ // T_MIX
    fwd = lambda s, i: s * nb + i
    bwd = lambda s, i: s * nb + (nb - 1 - i)

    def col(width, offset, blk):
        return pl.BlockSpec((T_MIX, width), lambda s, i: (blk(s, i), offset // width))

    def whole(a):
        return pl.BlockSpec(a.shape, lambda s, i: (0,) * a.ndim)

    in_specs = [col(WIDTH, COL_QA, fwd), col(WIDTH, COL_ZF, fwd), col(WIDTH, COL_IA, fwd),
                col(WIDTH, COL_QA, bwd), col(WIDTH, COL_ZB, bwd), col(WIDTH, COL_IA, bwd),
                col(B_KEY, COL_QB, fwd), col(B_KEY, COL_KB, fwd), col(WIDTH, COL_VB, fwd),
                col(LANES, COL_LR, fwd),
                col(B_KEY, COL_QB, bwd), col(B_KEY, COL_KB, bwd), col(WIDTH, COL_VB, bwd),
                col(LANES, COL_LR, bwd), whole(upw), whole(upb)]
    out_spec_f = pl.BlockSpec((T_MIX, WIDTH), lambda s, i: (fwd(s, i), 0))
    out_spec_b = pl.BlockSpec((T_MIX, WIDTH), lambda s, i: (bwd(s, i), 0))
    n_states = 2 * WIDTH // HEAD_V
    return pl.pallas_call(
        _gla_kernel,
        grid=(n_seq, nb),
        in_specs=in_specs,
        out_specs=[out_spec_f, out_spec_b, out_spec_f, out_spec_b],
        out_shape=[jax.ShapeDtypeStruct((n_tok, WIDTH), O_DTYPE)] * 4,
        scratch_shapes=[pltpu.VMEM((n_states, HEAD_V, LANES), F32), pltpu.VMEM((n_states, HEAD_V, LANES), F32),
                        pltpu.VMEM((2, T_MIX, B_KEY), F32)],
        compiler_params=pltpu.CompilerParams(
            dimension_semantics=("arbitrary", "arbitrary"), vmem_limit_bytes=VMEM_LIMIT),
        name="mix_gla",
    )(*([p] * 14), upw, upb)


def _rglru_kernel(xf_ref, xfp_ref, xfn_ref, xb_ref, xbp_ref, xbn_ref, cw_ref, cb_ref, gw_ref, gb_ref,
                  lam_ref, of_ref, ob_ref, slab_ref, a_ref, u_ref, carry_ref):
    i = pl.program_id(1)
    nb = pl.num_programs(1)

    @pl.when(i == 0)
    def _():
        carry_ref[...] = jnp.zeros_like(carry_ref)

    n_slabs = WIDTH // LANES
    top = T_MIX - SUBLANES

    def gather(start, stride):
        return jnp.concatenate([slab_ref[c, pl.ds(start, SUBLANES, stride=stride), :]
                                for c in range(n_slabs)], axis=1)

    for d in range(2):
        reverse = d == 1
        x_ref, xp_ref, xn_ref, o_ref = (xb_ref, xbp_ref, xbn_ref, ob_ref) if reverse else (
            xf_ref, xfp_ref, xfn_ref, of_ref)
        blk = (nb - 1 - i) if reverse else i
        xf = x_ref[...].astype(F32)
        for c in range(n_slabs):
            for s in range(SUBLANES):
                slab_ref[c, s * MIX_PITCH:s * MIX_PITCH + MIX_GROUPS, :] = xf[
                    s * MIX_GROUPS:(s + 1) * MIX_GROUPS, c * LANES:(c + 1) * LANES]
        groups = [gather(j, MIX_PITCH) for j in range(MIX_GROUPS)]
        x = jnp.concatenate(groups, axis=0)
        before = jnp.where(blk == 0, 0.0, xp_ref[...].astype(F32))
        after = jnp.where(blk == nb - 1, 0.0, xn_ref[...].astype(F32))
        down1 = jnp.concatenate([before[HALO - 1:HALO], groups[-1][:SUBLANES - 1]], axis=0)
        down2 = jnp.concatenate([before[HALO - 2:HALO - 1], groups[-2][:SUBLANES - 1]], axis=0)
        up1 = jnp.concatenate([groups[0][1:], after[0:1]], axis=0)
        x_m1 = jnp.concatenate([down1, x[:top]], axis=0)
        x_m2 = jnp.concatenate([down2, down1, x[:top - SUBLANES]], axis=0)
        x_p1 = jnp.concatenate([x[SUBLANES:], up1], axis=0)
        xc = (cb_ref[...] + x_m2 * cw_ref[0:1, :] + x_m1 * cw_ref[1:2, :] + x * cw_ref[2:3, :]
              + x_p1 * cw_ref[3:4, :])
        xc_bf = xc.astype(BF16)
        gates = []
        for g in range(2):
            halves = [jnp.dot(xc_bf[:, hh * 256:(hh + 1) * 256], gw_ref[d, g, hh],
                              preferred_element_type=F32) for hh in range(2)]
            gates.append(_sigmoid_tanh(jnp.concatenate(halves, axis=1) + gb_ref[d, g:g + 1, :]))
        r_gate, i_gate = gates
        lam = lam_ref[d:d + 1, :]
        softplus_neg_lam = jnp.maximum(-lam, 0.0) + jnp.log(1.0 + jnp.exp(-jnp.abs(lam)))
        log_a = -RG_C * r_gate * softplus_neg_lam
        a = jnp.exp(log_a)
        one_minus_a2 = -jnp.tanh(log_a) * (1.0 + a * a)
        root = jnp.where(one_minus_a2 > 0.0, one_minus_a2 * lax.rsqrt(one_minus_a2), 0.0)
        u = root * (i_gate * xc)
        a_ref[...] = a
        u_ref[...] = u

        def body(n, state):
            h_prev, p_prev = state
            j = (MIX_GROUPS - 1 - n) if reverse else n
            rows = pl.ds(pl.multiple_of(j * SUBLANES, SUBLANES), SUBLANES)
            a_j = a_ref[rows, :]
            h = a_j * h_prev + u_ref[rows, :]
            p = a_j * p_prev
            u_ref[rows, :] = h
            a_ref[rows, :] = p
            return h, p

        init = (jnp.zeros((SUBLANES, WIDTH), F32), jnp.ones((SUBLANES, WIDTH), F32))
        h_end, p_end = lax.fori_loop(0, MIX_GROUPS, body, init, unroll=8)
        carry = carry_ref[d][0:1, :]
        carries = [None] * SUBLANES
        for s in (range(SUBLANES - 1, -1, -1) if reverse else range(SUBLANES)):
            carries[s] = carry
            carry = h_end[s:s + 1, :] + p_end[s:s + 1, :] * carry
        carry_ref[d] = jnp.broadcast_to(carry, (SUBLANES, WIDTH))
        carry_in = jnp.concatenate(carries, axis=0)
        h_all = u_ref[...] + a_ref[...] * jnp.concatenate([carry_in] * MIX_GROUPS, axis=0)
        for c in range(n_slabs):
            slab_ref[c, 0:T_MIX, :] = h_all[:, c * LANES:(c + 1) * LANES]
        groups_per_sublane = MIX_GROUPS // SUBLANES
        for k2 in range(MIX_GROUPS // 2):
            pair = []
            for k in (2 * k2, 2 * k2 + 1):
                s, j0 = k // groups_per_sublane, (k % groups_per_sublane) * SUBLANES
                pair.append(gather(j0 * SUBLANES + s, SUBLANES))
            o_ref[2 * SUBLANES * k2:2 * SUBLANES * (k2 + 1), :] = jnp.concatenate(pair, axis=0).astype(
                o_ref.dtype)


def _rglru(p, cw, cb, gw, gb, lam):
    n_tok = p.shape[0]
    n_seq = n_tok // SEQ
    nb = SEQ // T_MIX
    halo = T_MIX // HALO
    last_halo = n_tok // HALO - 1
    xcol = COL_XC // WIDTH
    fwd = lambda s, i: s * nb + i
    bwd = lambda s, i: s * nb + (nb - 1 - i)

    def specs(blk):
        return [
            pl.BlockSpec((T_MIX, WIDTH), lambda s, i: (blk(s, i), xcol)),
            pl.BlockSpec((HALO, WIDTH), lambda s, i: (jnp.maximum(blk(s, i) * halo - 1, 0), xcol)),
            pl.BlockSpec((HALO, WIDTH),
                         lambda s, i: (jnp.minimum((blk(s, i) + 1) * halo, last_halo), xcol)),
        ]

    def whole(a):
        return pl.BlockSpec(a.shape, lambda s, i: (0,) * a.ndim)

    return pl.pallas_call(
        _rglru_kernel,
        grid=(n_seq, nb),
        in_specs=specs(fwd) + specs(bwd) + [whole(cw), whole(cb), whole(gw), whole(gb), whole(lam)],
        out_specs=[pl.BlockSpec((T_MIX, WIDTH), lambda s, i: (fwd(s, i), 0)),
                   pl.BlockSpec((T_MIX, WIDTH), lambda s, i: (bwd(s, i), 0))],
        out_shape=[jax.ShapeDtypeStruct((n_tok, WIDTH), O_DTYPE)] * 2,
        scratch_shapes=[pltpu.VMEM((WIDTH // LANES, SUBLANES * MIX_PITCH, LANES), F32),
                        pltpu.VMEM((T_MIX, WIDTH), F32),
                        pltpu.VMEM((T_MIX, WIDTH), F32),
                        pltpu.VMEM((2, SUBLANES, WIDTH), F32)],
        compiler_params=pltpu.CompilerParams(
            dimension_semantics=("arbitrary", "arbitrary"), vmem_limit_bytes=VMEM_LIMIT),
        name="mix_rglru",
    )(p, p, p, p, p, p, cw, cb, gw, gb, lam)


def _head_norm(o, w):
    parts = [_rms(o[:, h * HEAD_V:(h + 1) * HEAD_V], w) for h in range(WIDTH // HEAD_V)]
    return jnp.concatenate(parts, axis=1)


def _merge_kernel(x_ref, oaf_ref, oab_ref, oga_ref, obf_ref, obb_ref, ogb_ref, hcf_ref, hcb_ref, yc_ref,
                  ga_ref, gb_ref, gc_ref, nwa_ref, nwb_ref, wbr_ref, wout_ref, o_ref):
    ld = lambda ref: ref[...].astype(F32)
    o_a = _head_norm(ld(oaf_ref) + ld(oab_ref), nwa_ref[...]) * ld(oga_ref)
    o_b = _head_norm(ld(obf_ref) + ld(obb_ref), nwb_ref[...]) * ld(ogb_ref)
    o_c = (ld(hcf_ref) + ld(hcb_ref)) * ld(yc_ref)
    m = ld(ga_ref) * jnp.dot(o_a.astype(BF16), wbr_ref[0], preferred_element_type=F32)
    m = m + ld(gb_ref) * jnp.dot(o_b.astype(BF16), wbr_ref[1], preferred_element_type=F32)
    m = m + ld(gc_ref) * jnp.dot(o_c.astype(BF16), wbr_ref[2], preferred_element_type=F32)
    o_ref[...] = x_ref[...] + jnp.dot(m.astype(BF16), wout_ref[...], preferred_element_type=F32)


def _merge(x, p, oaf, oab, obf, obb, hcf, hcb, nwa, nwb, wbr, wout):
    n_tok = x.shape[0]
    tok = lambda width, offset: pl.BlockSpec((T_MERGE, width), lambda i: (i, offset // width))

    def whole(a):
        return pl.BlockSpec(a.shape, lambda i: (0,) * a.ndim, pipeline_mode=pl.Buffered(1))

    return pl.pallas_call(
        _merge_kernel,
        grid=(n_tok // T_MERGE,),
        in_specs=[tok(D_MODEL, 0), tok(WIDTH, 0), tok(WIDTH, 0), tok(WIDTH, COL_OGA),
                  tok(WIDTH, 0), tok(WIDTH, 0), tok(WIDTH, COL_OGB),
                  tok(WIDTH, 0), tok(WIDTH, 0), tok(WIDTH, COL_YC),
                  tok(D_MODEL, COL_GA), tok(D_MODEL, COL_GB), tok(D_MODEL, COL_GC),
                  whole(nwa), whole(nwb), whole(wbr), whole(wout)],
        out_specs=pl.BlockSpec((T_MERGE, D_MODEL), lambda i: (i, 0)),
        out_shape=jax.ShapeDtypeStruct((n_tok, D_MODEL), F32),
        compiler_params=pltpu.CompilerParams(
            dimension_semantics=("arbitrary",), vmem_limit_bytes=VMEM_LIMIT),
        name="merge",
    )(x, oaf, oab, p, obf, obb, p, hcf, hcb, p, p, p, p, nwa, nwb, wbr, wout)


def _ffn_kernel(x_ref, xp_ref, xn_ref, nw_ref, wup_ref, cw_ref, cb_ref, wd_ref, fw_ref, o_ref, h_ref,
                slab_ref, act_ref, *, final_norm):
    i = pl.program_id(0)
    blocks_per_seq = SEQ // T_FFN
    first = (i % blocks_per_seq) == 0
    last = (i % blocks_per_seq) == blocks_per_seq - 1
    nw = nw_ref[...]

    n_slabs = D_MODEL // LANES

    for c in range(n_slabs):
        for s in range(SUBLANES):
            slab_ref[c, s * SLAB_PITCH:s * SLAB_PITCH + FFN_GROUPS, :] = x_ref[
                s * FFN_GROUPS:(s + 1) * FFN_GROUPS, c * LANES:(c + 1) * LANES]

    def gather(start, stride):
        return jnp.concatenate([slab_ref[c, pl.ds(start, SUBLANES, stride=stride), :]
                                for c in range(n_slabs)], axis=1)

    for jj in range(FFN_GROUPS // 2):
        xb = jnp.concatenate([gather(2 * jj, SLAB_PITCH), gather(2 * jj + 1, SLAB_PITCH)], axis=0)
        h_ref[2 * SUBLANES * jj:2 * SUBLANES * (jj + 1), :] = _rms(xb, nw).astype(BF16)
    halo = jnp.concatenate([jnp.where(first, 0.0, _rms(xp_ref[...], nw)),
                            jnp.where(last, 0.0, _rms(xn_ref[...], nw))], axis=0)
    h_ref[T_FFN:, :] = halo.astype(BF16)

    def up(f):
        h = h_ref[...]
        return tuple(jnp.dot(h, wup_ref[:, c * D_FF + f * F_TILE:c * D_FF + (f + 1) * F_TILE],
                             preferred_element_type=F32) for c in range(2))

    def conv(u, c, f):
        cols = slice(c * D_FF + f * F_TILE, c * D_FF + (f + 1) * F_TILE)
        body = u[:T_FFN]
        before = u[T_FFN + SUBLANES - 1:T_FFN + SUBLANES]
        after = u[T_FFN + SUBLANES:T_FFN + SUBLANES + 1]
        prev0 = jnp.concatenate([before, body[T_FFN - SUBLANES:T_FFN - 1]], axis=0)
        next_last = jnp.concatenate([body[1:SUBLANES], after], axis=0)
        u_prev = jnp.concatenate([prev0, body[:T_FFN - SUBLANES]], axis=0)
        u_next = jnp.concatenate([body[SUBLANES:], next_last], axis=0)
        return (cb_ref[:, cols] + u_prev * cw_ref[0:1, cols] + body * cw_ref[1:2, cols]
                + u_next * cw_ref[2:3, cols])

    n_f = D_FF // F_TILE
    acc = None
    pending = [up(f) for f in range(FFN_UP_AHEAD)]
    for f in range(n_f):
        ug, uv = pending.pop(0)
        if f + FFN_UP_AHEAD < n_f:
            pending.append(up(f + FFN_UP_AHEAD))
        act = _gelu_tanh(conv(ug, 0, f)) * conv(uv, 1, f)
        act_ref[:, f * F_TILE:(f + 1) * F_TILE] = act.astype(BF16)
        if f + 1 in FFN_DOWN_SPLITS:
            lo = FFN_DOWN_SPLITS[FFN_DOWN_SPLITS.index(f + 1) - 1] * F_TILE
            down = jnp.dot(act_ref[:, lo:(f + 1) * F_TILE], wd_ref[lo:(f + 1) * F_TILE, :],
                           preferred_element_type=F32)
            acc = down if acc is None else acc + down

    for c in range(n_slabs):
        slab_ref[c, 0:T_FFN, :] = acc[:, c * LANES:(c + 1) * LANES]
    groups_per_sublane = FFN_GROUPS // SUBLANES
    for k in range(FFN_GROUPS):
        rows = slice(k * SUBLANES, (k + 1) * SUBLANES)
        s, j0 = k // groups_per_sublane, (k % groups_per_sublane) * SUBLANES
        y = x_ref[rows, :] + gather(j0 * SUBLANES + s, SUBLANES)
        o_ref[rows, :] = _rms(y, fw_ref[...]) if final_norm else y


def _ffn(x, nw, w_up, cw, cb, w_down, fw, final_norm):
    n_tok = x.shape[0]
    halo = T_FFN // SUBLANES
    last_halo = n_tok // SUBLANES - 1

    def resident(a):
        return pl.BlockSpec(a.shape, lambda i: (0,) * a.ndim, pipeline_mode=pl.Buffered(1))

    return pl.pallas_call(
        functools.partial(_ffn_kernel, final_norm=final_norm),
        grid=(n_tok // T_FFN,),
        in_specs=[
            pl.BlockSpec((T_FFN, D_MODEL), lambda i: (i, 0)),
            pl.BlockSpec((SUBLANES, D_MODEL), lambda i: (jnp.maximum(i * halo - 1, 0), 0)),
            pl.BlockSpec((SUBLANES, D_MODEL), lambda i: (jnp.minimum((i + 1) * halo, last_halo), 0)),
            resident(nw), resident(w_up), resident(cw), resident(cb), resident(w_down), resident(fw),
        ],
        out_specs=pl.BlockSpec((T_FFN, D_MODEL), lambda i: (i, 0)),
        out_shape=jax.ShapeDtypeStruct((n_tok, D_MODEL), F32),
        scratch_shapes=[pltpu.VMEM((T_FFN + 2 * SUBLANES, D_MODEL), BF16),
                        pltpu.VMEM((D_MODEL // LANES, SUBLANES * SLAB_PITCH, LANES), F32),
                        pltpu.VMEM((T_FFN, D_FF), BF16)],
        compiler_params=pltpu.CompilerParams(
            dimension_semantics=("arbitrary",), vmem_limit_bytes=VMEM_LIMIT),
        name="ffn",
    )(x, x, x, nw, w_up, cw, cb, w_down, fw)


def _layout_w_in(w_in_l):
    (q_a, zf_a, zb_a, i_a, og_a, q_b, k_b, v_b, og_b, lrf, lrb, x_c, y_c, g_a, g_b, g_c) = jnp.split(
        w_in_l.astype(BF16),
        [512, 1024, 1536, 2048, 2560, 2816, 3072, 3584, 4096, 4112, 4128, 4640, 5152, 6176, 7200], axis=1)
    pad = jnp.zeros((D_MODEL, LANES - 2 * GLA_RANK), BF16)
    q_b = q_b * GLA_Q_SCALE
    return jnp.concatenate([q_a, zf_a, zb_a, i_a, og_a, q_b, k_b, v_b, og_b, x_c, y_c, g_a, g_b, g_c,
                            lrf, lrb, pad], axis=1)


def _layout_gla_up(up_w_l):
    out = jnp.zeros((2, LANES, B_KEY), F32)
    out = out.at[0, 0:GLA_RANK].set(up_w_l[0])
    out = out.at[1, GLA_RANK:2 * GLA_RANK].set(up_w_l[1])
    return out.astype(BF16)


def _layout_rglru_gates(wa_l, wx_l):
    def tiles(w):
        w = w.reshape(2, 2, 4, C_BLOCK, C_BLOCK)
        eye = jnp.eye(4, dtype=w.dtype)
        t = jnp.einsum('dhbkj,bc->dhbkcj', w, eye)
        return t.reshape(2, 2, 4 * C_BLOCK, 4 * C_BLOCK)
    return jnp.stack([tiles(wa_l), tiles(wx_l)], axis=1).astype(BF16)


def kernel(x_prompt, x_sample, norm_mix_w, w_in, hgrn_lb_logits, hgrn_norm_w, gla_up_w, gla_up_b,
           gla_norm_w, c_conv_w, c_conv_b, rglru_wa, rglru_ba, rglru_wx, rglru_bx, rglru_lam,
           w_branch, w_out, norm_ffn_w, ffn_up, ffn_conv_w, ffn_conv_b, ffn_down, final_norm_w):
    lb_all = jnp.cumsum(jax.nn.softmax(hgrn_lb_logits.astype(F32), axis=0), axis=0)
    lb_all = lb_all - lb_all[0]
    fw = final_norm_w.reshape(1, D_MODEL)

    xs = [x_prompt.reshape(-1, D_MODEL), x_sample.reshape(-1, D_MODEL)]
    for l in range(DEPTH):
        w_in_l = _layout_w_in(w_in[l])
        up_w_l = _layout_gla_up(gla_up_w[l])
        gate_w_l = _layout_rglru_gates(rglru_wa[l], rglru_wx[l])
        gate_b_l = jnp.stack([rglru_ba[l], rglru_bx[l]], axis=1)
        w_branch_l, w_out_l = w_branch[l].astype(BF16), w_out[l].astype(BF16)
        ffn_up_l, ffn_down_l = ffn_up[l].astype(BF16), ffn_down[l].astype(BF16)
        for n, x in enumerate(xs):
            p = _proj(x, norm_mix_w[l].reshape(1, D_MODEL), lb_all[l], w_in_l)
            oaf, oab, obf, obb = _bidir_gla(p, up_w_l, gla_up_b[l].reshape(2, 1, B_KEY))
            hcf, hcb = _rglru(p, c_conv_w[l], c_conv_b[l].reshape(1, WIDTH), gate_w_l, gate_b_l,
                              rglru_lam[l])
            x = _merge(x, p, oaf, oab, obf, obb, hcf, hcb,
                       hgrn_norm_w[l].reshape(1, HEAD_V), gla_norm_w[l].reshape(1, HEAD_V),
                       w_branch_l, w_out_l)
            xs[n] = _ffn(x, norm_ffn_w[l].reshape(1, D_MODEL), ffn_up_l, ffn_conv_w[l],
                         ffn_conv_b[l].reshape(1, 2 * D_FF), ffn_down_l, fw,
                         final_norm=(l == DEPTH - 1))

    return (xs[0].reshape(x_prompt.shape), xs[1].reshape(x_sample.shape))
```
